```python
import math
import jax, jax.numpy as jnp
from jax import lax
import numpy as np

D_MODEL = 4096
BATCH = 2
SEQ = 4096
DEPTH = 2

HEAD_DIM = 128
D_MIX = D_MODEL
N_HEADS_A = D_MIX // (2 * HEAD_DIM)
WIDTH_A = N_HEADS_A * HEAD_DIM
DIL_CONFIGS = ((128, 1), (512, 4), (2048, 16))
N_HEADS_B = (D_MIX - WIDTH_A) // HEAD_DIM
QK_NOPE_DIM = 128
QK_ROPE_DIM = 64
V_HEAD_DIM = 128
WIDTH_B = N_HEADS_B * V_HEAD_DIM
Q_LORA_RANK = 1024
KV_LORA_RANK = 512
ROPE_THETA = 10000.0
P_IN = 3 * WIDTH_A + Q_LORA_RANK + KV_LORA_RANK + QK_ROPE_DIM
D_FF = 256 * (-(-(8 * D_MODEL) // (3 * 256)))
NORM_EPS = 1e-6
Q_BLOCK = 128

kernel_name = "hybrid_dilated_mla_macaron"


def rms_norm(x, g):
    xf = x.astype(jnp.float32)
    y = xf * lax.rsqrt(jnp.mean(xf * xf, axis=-1, keepdims=True) + NORM_EPS)
    return y.astype(x.dtype) * g


def swiglu(x, w_gate, w_up, w_down):
    return (jax.nn.silu(x @ w_gate) * (x @ w_up)) @ w_down


def alibi_slopes(n_heads):
    return jnp.asarray(np.array([2.0 ** (-8.0 * (i + 1) / n_heads) for i in range(n_heads)], dtype=np.float32))


def rope(x, cos, sin):
    half = x.shape[-1] // 2
    x1, x2 = x[..., :half], x[..., half:]
    return jnp.concatenate([x1 * cos - x2 * sin, x2 * cos + x1 * sin], axis=-1)


def dilated_branch(q, k, v, slopes, window, dilation):
    B, H, S, Dh = q.shape
    span = window // dilation
    qb_len = span
    L = -(-S // dilation)
    nb = -(-L // qb_len)
    s_pad = nb * qb_len * dilation

    def to_blocks(a):
        a = jnp.pad(a, ((0, 0), (0, 0), (0, s_pad - S), (0, 0)))
        a = a.reshape(B, H, nb * qb_len, dilation, Dh).transpose(0, 1, 3, 2, 4)
        return a.reshape(B, H, dilation, nb, qb_len, Dh)

    def with_prev(a):
        prev = jnp.pad(a[:, :, :, :-1], ((0, 0), (0, 0), (0, 0), (1, 0), (0, 0), (0, 0)))
        return jnp.concatenate([prev, a], axis=4)

    qb = to_blocks(q)
    kw = with_prev(to_blocks(k))
    vw = with_prev(to_blocks(v))

    i = jnp.arange(qb_len)[:, None]
    j = jnp.arange(2 * qb_len)[None, :]
    du = i - j + qb_len
    blk = jnp.arange(nb)[:, None, None]
    valid = (du >= 0) & (du <= span) & (blk * qb_len - qb_len + j >= 0)
    bias = -slopes[:, None, None] * (du * dilation).astype(jnp.float32)

    s = jnp.einsum('bhrnqd,bhrnkd->bhrnqk', qb, kw) + bias[None, :, None, None]
    s = jnp.where(valid[None, None, None], s, -jnp.inf)
    m = jnp.max(s, axis=-1, keepdims=True)
    p = jnp.exp(s - m)
    den = jnp.sum(p, axis=-1, keepdims=True)
    o = jnp.einsum('bhrnqk,bhrnkd->bhrnqd', p, vw) / den
    lse = (m + jnp.log(den))[..., 0]

    o = o.reshape(B, H, dilation, nb * qb_len, Dh).transpose(0, 1, 3, 2, 4).reshape(B, H, s_pad, Dh)[:, :, :S]
    lse = lse.reshape(B, H, dilation, nb * qb_len).transpose(0, 1, 3, 2).reshape(B, H, s_pad)[:, :, :S]
    return o, lse


def dilated_attention(q, k, v, slopes):
    outs, lses = [], []
    for window, dilation in DIL_CONFIGS:
        o, l = dilated_branch(q, k, v, slopes, window, dilation)
        outs.append(o)
        lses.append(l)
    w = jax.nn.softmax(jnp.stack(lses, axis=0), axis=0)
    return jnp.sum(w[..., None] * jnp.stack(outs, axis=0), axis=0)


def mla_attention(q_nope, q_rope, k_nope, k_rope, v):
    B, S, H, _ = q_nope.shape
    nq = S // Q_BLOCK
    qn = q_nope.reshape(B, nq, Q_BLOCK, H, QK_NOPE_DIM).transpose(1, 0, 2, 3, 4)
    qr = q_rope.reshape(B, nq, Q_BLOCK, H, QK_ROPE_DIM).transpose(1, 0, 2, 3, 4)
    kpos = jnp.arange(S)

    def block(args):
        idx, qn_b, qr_b = args
        s = jnp.einsum('bqhd,bkhd->bhqk', qn_b, k_nope) + jnp.einsum('bqhd,bkd->bhqk', qr_b, k_rope)
        qpos = idx * Q_BLOCK + jnp.arange(Q_BLOCK)
        s = jnp.where(kpos[None, :] <= qpos[:, None], s, -jnp.inf)
        p = jax.nn.softmax(s, axis=-1)
        return jnp.einsum('bhqk,bkhd->bqhd', p, v)

    o = lax.map(block, (jnp.arange(nq), qn, qr))
    return o.transpose(1, 0, 2, 3, 4).reshape(B, S, H * V_HEAD_DIM)


def hybrid_layer(x, cos, sin, slopes,
                 ffn1_norm, ffn1_w_gate, ffn1_w_up, ffn1_w_down,
                 mix_norm, w_in, g_cq, w_uq, g_ckv, w_ukv, g_out_a, g_out_b, w_o,
                 ffn2_norm, ffn2_w_gate, ffn2_w_up, ffn2_w_down):
    B, S, _ = x.shape
    f32 = jnp.float32
    h = x + 0.5 * swiglu(rms_norm(x, ffn1_norm), ffn1_w_gate, ffn1_w_up, ffn1_w_down)

    n = rms_norm(h, mix_norm)
    proj = n @ w_in
    cuts = list(np.cumsum([WIDTH_A, WIDTH_A, WIDTH_A, Q_LORA_RANK, KV_LORA_RANK]))
    qa, ka, va, c_q, c_kv, k_rope = jnp.split(proj, cuts, axis=-1)

    def heads_a(t):
        return t.reshape(B, S, N_HEADS_A, HEAD_DIM).transpose(0, 2, 1, 3).astype(f32)
    out_a = dilated_attention(heads_a(qa) * (HEAD_DIM ** -0.5), heads_a(ka), heads_a(va), slopes)
    out_a = out_a.transpose(0, 2, 1, 3).reshape(B, S, WIDTH_A).astype(x.dtype)

    q = (rms_norm(c_q, g_cq) @ w_uq).reshape(B, S, N_HEADS_B, QK_NOPE_DIM + QK_ROPE_DIM).astype(f32)
    kv = (rms_norm(c_kv, g_ckv) @ w_ukv).reshape(B, S, N_HEADS_B, QK_NOPE_DIM + V_HEAD_DIM).astype(f32)
    scale = (QK_NOPE_DIM + QK_ROPE_DIM) ** -0.5
    q_nope = q[..., :QK_NOPE_DIM] * scale
    q_rope = rope(q[..., QK_NOPE_DIM:], cos[:, :, None, :], sin[:, :, None, :]) * scale
    k_nope = kv[..., :QK_NOPE_DIM]
    v_b = kv[..., QK_NOPE_DIM:]
    k_r = rope(k_rope.astype(f32), cos, sin)
    out_b = mla_attention(q_nope, q_rope, k_nope, k_r, v_b).astype(x.dtype)

    mixed = jnp.concatenate([rms_norm(out_a, g_out_a), rms_norm(out_b, g_out_b)], axis=-1)
    h = h + mixed @ w_o

    h = h + 0.5 * swiglu(rms_norm(h, ffn2_norm), ffn2_w_gate, ffn2_w_up, ffn2_w_down)
    return h


def setup_inputs(seed: int = 0) -> dict:
    key = jax.random.key(seed)
    ks = jax.random.split(key, 24)
    f32 = jnp.float32

    def w(k, shape, fan_in):
        return jax.random.normal(k, shape, f32) * (fan_in ** -0.5)

    def gain(k, shape):
        return 1.0 + 0.02 * jax.random.normal(k, shape, f32)

    x = jax.random.normal(ks[0], (BATCH, SEQ, D_MODEL), f32)
    offset = jax.random.randint(ks[1], (BATCH, 1), 0, 1024, dtype=jnp.int32)
    positions = (offset + jnp.arange(SEQ, dtype=jnp.int32)[None, :]).astype(jnp.int32)
    return {
        "x": x,
        "positions": positions,
        "ffn1_norm": gain(ks[2], (DEPTH, D_MODEL)),
        "ffn1_w_gate": w(ks[3], (DEPTH, D_MODEL, D_FF), D_MODEL),
        "ffn1_w_up": w(ks[4], (DEPTH, D_MODEL, D_FF), D_MODEL),
        "ffn1_w_down": w(ks[5], (DEPTH, D_FF, D_MODEL), D_FF),
        "mix_norm": gain(ks[6], (DEPTH, D_MODEL)),
        "w_in": w(ks[7], (DEPTH, D_MODEL, P_IN), D_MODEL),
        "g_cq": gain(ks[8], (DEPTH, Q_LORA_RANK)),
        "w_uq": w(ks[9], (DEPTH, Q_LORA_RANK, N_HEADS_B * (QK_NOPE_DIM + QK_ROPE_DIM)), Q_LORA_RANK),
        "g_ckv": gain(ks[10], (DEPTH, KV_LORA_RANK)),
        "w_ukv": w(ks[11], (DEPTH, KV_LORA_RANK, N_HEADS_B * (QK_NOPE_DIM + V_HEAD_DIM)), KV_LORA_RANK),
        "g_out_a": gain(ks[12], (DEPTH, WIDTH_A)),
        "g_out_b": gain(ks[13], (DEPTH, WIDTH_B)),
        "w_o": w(ks[14], (DEPTH, WIDTH_A + WIDTH_B, D_MODEL), WIDTH_A + WIDTH_B),
        "ffn2_norm": gain(ks[15], (DEPTH, D_MODEL)),
        "ffn2_w_gate": w(ks[16], (DEPTH, D_MODEL, D_FF), D_MODEL),
        "ffn2_w_up": w(ks[17], (DEPTH, D_MODEL, D_FF), D_MODEL),
        "ffn2_w_down": w(ks[18], (DEPTH, D_FF, D_MODEL), D_FF),
        "final_norm": gain(ks[19], (D_MODEL,)),
    }


def reference(x, positions, ffn1_norm, ffn1_w_gate, ffn1_w_up, ffn1_w_down,
              mix_norm, w_in, g_cq, w_uq, g_ckv, w_ukv, g_out_a, g_out_b, w_o,
              ffn2_norm, ffn2_w_gate, ffn2_w_up, ffn2_w_down, final_norm):
    inv_freq = 1.0 / (ROPE_THETA ** (jnp.arange(0, QK_ROPE_DIM, 2, dtype=jnp.float32) / QK_ROPE_DIM))
    ang = positions.astype(jnp.float32)[..., None] * inv_freq
    cos, sin = jnp.cos(ang), jnp.sin(ang)
    slopes = alibi_slopes(N_HEADS_A)

    h = x
    for l in range(DEPTH):
        h = hybrid_layer(h, cos, sin, slopes,
                         ffn1_norm[l], ffn1_w_gate[l], ffn1_w_up[l], ffn1_w_down[l],
                         mix_norm[l], w_in[l], g_cq[l], w_uq[l], g_ckv[l], w_ukv[l],
                         g_out_a[l], g_out_b[l], w_o[l],
                         ffn2_norm[l], ffn2_w_gate[l], ffn2_w_up[l], ffn2_w_down[l])
    return rms_norm(h, final_norm)
```

```python
import functools
import math

import numpy as np
import jax
import jax.numpy as jnp
from jax import lax
from jax.experimental import pallas as pl
from jax.experimental.pallas import tpu as pltpu

F32 = jnp.float32
BF16 = jnp.bfloat16

D_MODEL = 4096
HEAD_DIM = 128
N_HEADS = 16
WIDTH = N_HEADS * HEAD_DIM
DIL_CONFIGS = ((128, 1), (512, 4), (2048, 16))
ROPE_DIM = 64
Q_LORA = 1024
KV_LORA = 512
ROPE_THETA = 10000.0
EPS = 1e-6
MASK = -1e30

VMEM_LIMIT = 56 * 1024 * 1024
ATT_TQ = 256
ATT_TK = 256
DIL_MAX_DIST = max(w for w, _ in DIL_CONFIGS)
DIL_NOFF = DIL_MAX_DIST // ATT_TK + 1


def _params(n_axes):
    return pltpu.CompilerParams(
        dimension_semantics=("arbitrary",) * n_axes, vmem_limit_bytes=VMEM_LIMIT)


def _rmsnorm_kernel(x_ref, g_ref, o_ref):
    x = x_ref[...]
    ms = jnp.mean(x * x, axis=-1, keepdims=True)
    o_ref[...] = ((x * lax.rsqrt(ms + EPS)) * g_ref[...]).astype(o_ref.dtype)


def rmsnorm(x, g, out_dtype, tm=256):
    m, d = x.shape
    return pl.pallas_call(
        _rmsnorm_kernel,
        grid=(m // tm,),
        in_specs=[pl.BlockSpec((tm, d), lambda i: (i, 0)),
                  pl.BlockSpec((1, d), lambda i: (0, 0))],
        out_specs=pl.BlockSpec((tm, d), lambda i: (i, 0)),
        out_shape=jax.ShapeDtypeStruct((m, d), out_dtype),
        compiler_params=_params(1),
        name="rmsnorm",
    )(x, g.reshape(1, d))


def _outnorm_kernel(a_ref, b_ref, ga_ref, gb_ref, o_ref):
    def norm(x, g):
        ms = jnp.mean(x * x, axis=-1, keepdims=True)
        return ((x * lax.rsqrt(ms + EPS)) * g).astype(o_ref.dtype)
    o_ref[:, :WIDTH] = norm(a_ref[...], ga_ref[...])
    o_ref[:, WIDTH:] = norm(b_ref[...], gb_ref[...])


def outnorm(out_a, out_b, g_a, g_b, tm=256):
    m = out_a.shape[0]
    return pl.pallas_call(
        _outnorm_kernel,
        grid=(m // tm,),
        in_specs=[pl.BlockSpec((tm, WIDTH), lambda i: (i, 0)),
                  pl.BlockSpec((tm, WIDTH), lambda i: (i, 0)),
                  pl.BlockSpec((1, WIDTH), lambda i: (0, 0)),
                  pl.BlockSpec((1, WIDTH), lambda i: (0, 0))],
        out_specs=pl.BlockSpec((tm, 2 * WIDTH), lambda i: (i, 0)),
        out_shape=jax.ShapeDtypeStruct((m, 2 * WIDTH), BF16),
        compiler_params=_params(1),
        name="outnorm",
    )(out_a, out_b, g_a.reshape(1, WIDTH), g_b.reshape(1, WIDTH))


def _dot(a, w):
    return jnp.dot(a, w.astype(BF16), preferred_element_type=F32)


def _accumulate(acc_ref, a_ref, w_ref, nk, k_rem):
    k = pl.program_id(2)
    tk = a_ref.shape[1]

    @pl.when(k == 0)
    def _():
        acc_ref[...] = jnp.zeros_like(acc_ref)

    if k_rem == tk:
        acc_ref[...] += _dot(a_ref[...], w_ref[...])
    else:
        @pl.when(k < nk - 1)
        def _():
            acc_ref[...] += _dot(a_ref[...], w_ref[...])

        @pl.when(k == nk - 1)
        def _():
            a = a_ref[...]
            w = w_ref[...]
            col = lax.broadcasted_iota(jnp.int32, a.shape, 1)
            row = lax.broadcasted_iota(jnp.int32, w.shape, 0)
            a = jnp.where(col < k_rem, a, jnp.zeros_like(a))
            w = jnp.where(row < k_rem, w, jnp.zeros_like(w))
            acc_ref[...] += _dot(a, w)


def _mm_kernel(a_ref, w_ref, o_ref, acc_ref, *, nk, k_rem, scale, n_scaled):
    _accumulate(acc_ref, a_ref, w_ref, nk, k_rem)

    @pl.when(pl.program_id(2) == nk - 1)
    def _():
        s = jnp.where(pl.program_id(0) < n_scaled, jnp.float32(scale), jnp.float32(1.0))
        o_ref[...] = (acc_ref[...] * s).astype(o_ref.dtype)


def _mm_res_kernel(a_ref, w_ref, r_ref, o_ref, acc_ref, *, nk, k_rem, scale):
    _accumulate(acc_ref, a_ref, w_ref, nk, k_rem)

    @pl.when(pl.program_id(2) == nk - 1)
    def _():
        o_ref[...] = r_ref[...] + scale * acc_ref[...]


def matmul(a, w, layer, *, out_dtype, n_cols=None, col_block0=0, scale=1.0, n_scaled=0,
           residual=None, tm=1024, tn=1024, tk=1024):
    m, kdim = a.shape
    n = w.shape[-1] if n_cols is None else n_cols
    tk = min(tk, kdim)
    nk = pl.cdiv(kdim, tk)
    k_rem = kdim - (nk - 1) * tk
    grid = (pl.cdiv(n, tn), m // tm, nk)
    a_spec = pl.BlockSpec((tm, tk), lambda j, i, k: (i, k))
    if w.ndim == 3:
        w_spec = pl.BlockSpec((None, tk, tn), lambda j, i, k: (layer, k, j + col_block0))
    else:
        w_spec = pl.BlockSpec((tk, tn), lambda j, i, k: (k, j + col_block0))
    o_spec = pl.BlockSpec((tm, tn), lambda j, i, k: (i, j))
    if residual is None:
        body = functools.partial(_mm_kernel, nk=nk, k_rem=k_rem, scale=scale, n_scaled=n_scaled)
        in_specs, args = [a_spec, w_spec], (a, w)
    else:
        body = functools.partial(_mm_res_kernel, nk=nk, k_rem=k_rem, scale=scale)
        in_specs, args = [a_spec, w_spec, o_spec], (a, w, residual)
    return pl.pallas_call(
        body,
        grid=grid,
        in_specs=in_specs,
        out_specs=o_spec,
        out_shape=jax.ShapeDtypeStruct((m, n), out_dtype),
        scratch_shapes=[pltpu.VMEM((tm, tn), F32)],
        compiler_params=_params(3),
        name="matmul_res" if residual is not None else "matmul",
    )(*args)


def _ffn_up_kernel(a_ref, wg_ref, wu_ref, o_ref, accg_ref, accu_ref, *, nk):
    k = pl.program_id(2)

    @pl.when(k == 0)
    def _():
        accg_ref[...] = jnp.zeros_like(accg_ref)
        accu_ref[...] = jnp.zeros_like(accu_ref)

    a = a_ref[...]
    accg_ref[...] += _dot(a, wg_ref[...])
    accu_ref[...] += _dot(a, wu_ref[...])

    @pl.when(k == nk - 1)
    def _():
        g = accg_ref[...]
        o_ref[...] = (g * jax.nn.sigmoid(g) * accu_ref[...]).astype(o_ref.dtype)


def ffn_up(a, w_gate, w_up, layer, tm=1024, tn=1024, tk=1024):
    m, kdim = a.shape
    n = w_gate.shape[-1]
    tk = min(tk, kdim)
    nk = kdim // tk
    w_spec = pl.BlockSpec((None, tk, tn), lambda j, i, k: (layer, k, j))
    return pl.pallas_call(
        functools.partial(_ffn_up_kernel, nk=nk),
        grid=(pl.cdiv(n, tn), m // tm, nk),
        in_specs=[pl.BlockSpec((tm, tk), lambda j, i, k: (i, k)), w_spec, w_spec],
        out_specs=pl.BlockSpec((tm, tn), lambda j, i, k: (i, j)),
        out_shape=jax.ShapeDtypeStruct((m, n), BF16),
        scratch_shapes=[pltpu.VMEM((tm, tn), F32), pltpu.VMEM((tm, tn), F32)],
        compiler_params=_params(3),
        name="ffn_up",
    )(a, w_gate, w_up)


def _rope_slab(x, cos_p, sin_p):
    lane = lax.broadcasted_iota(jnp.int32, x.shape, 1)
    half = ROPE_DIM // 2
    swapped = jnp.where(lane < half, pltpu.roll(x, 128 - half, 1), pltpu.roll(x, half, 1))
    return x * cos_p + swapped * sin_p


def _proj_c_kernel(a_ref, w_ref, gq_ref, gkv_ref, cos_ref, sin_ref,
                   cq_ref, ckv_ref, kr_ref, acc_ref, *, nk):
    _accumulate(acc_ref, a_ref, w_ref, nk, a_ref.shape[1])

    @pl.when(pl.program_id(2) == nk - 1)
    def _():
        def norm(x, g):
            ms = jnp.mean(x * x, axis=-1, keepdims=True)
            return (x * lax.rsqrt(ms + EPS)) * g
        cq_ref[...] = norm(acc_ref[:, :Q_LORA], gq_ref[...]).astype(cq_ref.dtype)
        ckv_ref[...] = norm(acc_ref[:, Q_LORA:Q_LORA + KV_LORA], gkv_ref[...]).astype(ckv_ref.dtype)
        kr = acc_ref[:, Q_LORA + KV_LORA:]
        kr_ref[...] = _rope_slab(kr, cos_ref[...], sin_ref[...]).astype(kr_ref.dtype)


def proj_c(a, w_c, g_cq, g_ckv, cos_p, sin_p, tm=512, tk=1024):
    m, kdim = a.shape
    n = w_c.shape[1]
    tk = min(tk, kdim)
    nk = kdim // tk
    row = lambda width: pl.BlockSpec((tm, width), lambda j, i, k: (i, 0))
    vec = lambda width: pl.BlockSpec((1, width), lambda j, i, k: (0, 0))
    return pl.pallas_call(
        functools.partial(_proj_c_kernel, nk=nk),
        grid=(1, m // tm, nk),
        in_specs=[pl.BlockSpec((tm, tk), lambda j, i, k: (i, k)),
                  pl.BlockSpec((tk, n), lambda j, i, k: (k, 0)),
                  vec(Q_LORA), vec(KV_LORA), row(128), row(128)],
        out_specs=[row(Q_LORA), row(KV_LORA), row(128)],
        out_shape=[jax.ShapeDtypeStruct((m, Q_LORA), BF16),
                   jax.ShapeDtypeStruct((m, KV_LORA), BF16),
                   jax.ShapeDtypeStruct((m, 128), BF16)],
        scratch_shapes=[pltpu.VMEM((tm, n), F32)],
        compiler_params=_params(3),
        name="proj_c",
    )(a, w_c, g_cq.reshape(1, Q_LORA), g_ckv.reshape(1, KV_LORA), cos_p, sin_p)


def _q_up_kernel(a_ref, w_ref, cos_ref, sin_ref, o_ref, *, scale, n_nope_tiles):
    x = _dot(a_ref[...], w_ref[...])
    j = pl.program_id(0)

    @pl.when(j < n_nope_tiles)
    def _():
        o_ref[...] = (x * scale).astype(o_ref.dtype)

    @pl.when(j >= n_nope_tiles)
    def _():
        cos_p = cos_ref[...]
        sin_p = sin_ref[...]
        for s in range(x.shape[1] // 128):
            slab = x[:, s * 128:(s + 1) * 128]
            o_ref[:, s * 128:(s + 1) * 128] = (_rope_slab(slab, cos_p, sin_p) * scale).astype(o_ref.dtype)


def q_up(a, w_q, cos_p, sin_p, scale, tm=1024, tn=1024):
    m, kdim = a.shape
    n = w_q.shape[1]
    return pl.pallas_call(
        functools.partial(_q_up_kernel, scale=scale, n_nope_tiles=WIDTH // tn),
        grid=(n // tn, m // tm),
        in_specs=[pl.BlockSpec((tm, kdim), lambda j, i: (i, 0)),
                  pl.BlockSpec((kdim, tn), lambda j, i: (0, j)),
                  pl.BlockSpec((tm, 128), lambda j, i: (i, 0)),
                  pl.BlockSpec((tm, 128), lambda j, i: (i, 0))],
        out_specs=pl.BlockSpec((tm, tn), lambda j, i: (i, j)),
        out_shape=jax.ShapeDtypeStruct((m, n), BF16),
        compiler_params=_params(2),
        name="q_up",
    )(a, w_q, cos_p, sin_p)


def _qk(q, k):
    return lax.dot_general(q, k, (((1,), (1,)), ((), ())), preferred_element_type=F32)


def _softmax_step(s, v, m, l, acc):
    m_new = jnp.maximum(m, jnp.max(s, axis=-1, keepdims=True))
    alpha = jnp.exp(m - m_new)
    p = jnp.exp(s - m_new)
    l = alpha * l + jnp.sum(p, axis=-1, keepdims=True)
    acc = alpha * acc + jnp.dot(p.astype(BF16), v, preferred_element_type=F32)
    return m_new, l, acc


def _softmax_first(s, v):
    m = jnp.max(s, axis=-1, keepdims=True)
    p = jnp.exp(s - m)
    l = jnp.sum(p, axis=-1, keepdims=True)
    acc = jnp.dot(p.astype(BF16), v, preferred_element_type=F32)
    return m, l, acc


def _mla_kernel(qn_ref, qr_ref, kn_ref, kr_ref, v_ref, o_ref, kc_ref):
    qi = pl.program_id(2)

    @pl.when(qi == 0)
    def _():
        kc_ref[:, :HEAD_DIM] = kn_ref[...]
        kc_ref[:, HEAD_DIM:] = kr_ref[...]

    q = jnp.concatenate([qn_ref[...], qr_ref[...]], axis=-1)
    row = lax.broadcasted_iota(jnp.int32, (ATT_TQ, ATT_TK), 0)
    col = lax.broadcasted_iota(jnp.int32, (ATT_TQ, ATT_TK), 1)

    k0 = pl.multiple_of(qi * ATT_TK, ATT_TK)
    s = jnp.where(row >= col, _qk(q, kc_ref[pl.ds(k0, ATT_TK), :]), MASK)
    carry = _softmax_first(s, v_ref[pl.ds(k0, ATT_TK), :])

    def body(j, carry):
        kj = pl.multiple_of(j * ATT_TK, ATT_TK)
        s = _qk(q, kc_ref[pl.ds(kj, ATT_TK), :])
        return _softmax_step(s, v_ref[pl.ds(kj, ATT_TK), :], *carry)

    m, l, acc = lax.fori_loop(0, qi, body, carry)
    o_ref[...] = acc / l


def mla_attention(q_all, kv, k_r, batch, seq):
    nq = seq // ATT_TQ
    return pl.pallas_call(
        _mla_kernel,
        grid=(batch, N_HEADS, nq),
        in_specs=[pl.BlockSpec((ATT_TQ, HEAD_DIM), lambda b, h, i: (b * nq + i, h)),
                  pl.BlockSpec((ATT_TQ, HEAD_DIM), lambda b, h, i: (b * nq + i, N_HEADS + h)),
                  pl.BlockSpec((seq, HEAD_DIM), lambda b, h, i: (b, 2 * h)),
                  pl.BlockSpec((seq, HEAD_DIM), lambda b, h, i: (b, 0)),
                  pl.BlockSpec((seq, HEAD_DIM), lambda b, h, i: (b, 2 * h + 1))],
        out_specs=pl.BlockSpec((ATT_TQ, HEAD_DIM), lambda b, h, i: (b * nq + i, h)),
        out_shape=jax.ShapeDtypeStruct((batch * seq, WIDTH), F32),
        scratch_shapes=[pltpu.VMEM((seq, 2 * HEAD_DIM), BF16)],
        compiler_params=_params(3),
        name="mla_attention",
    )(q_all, q_all, kv, k_r, kv)


def _dilated_log_mult_table():
    off = np.arange(DIL_NOFF)[:, None, None]
    i = np.arange(ATT_TQ)[None, :, None]
    j = np.arange(ATT_TK)[None, None, :]
    dist = off * ATT_TK + i - j
    mult = np.zeros(dist.shape, np.int32)
    for window, dilation in DIL_CONFIGS:
        mult += (dist >= 0) & (dist <= window) & (dist % dilation == 0)
    table = np.where(mult > 0, np.log(np.maximum(mult, 1).astype(np.float64)), MASK)
    return table.astype(np.float32)


def _dilated_kernel(slope_ref, q_ref, k_ref, v_ref, t_ref, o_ref):
    h = pl.program_id(1)
    qi = pl.program_id(2)
    neg_slope = -slope_ref[h]
    q = q_ref[...]
    row = lax.broadcasted_iota(jnp.int32, (ATT_TQ, ATT_TK), 0)
    col = lax.broadcasted_iota(jnp.int32, (ATT_TQ, ATT_TK), 1)
    alibi0 = neg_slope * (row - col).astype(F32)

    def scores(off):
        kj = pl.multiple_of((qi - off) * ATT_TK, ATT_TK)
        shift = neg_slope * (off * ATT_TK).astype(F32)
        s = _qk(q, k_ref[pl.ds(kj, ATT_TK), :]) + alibi0 + (t_ref[off] + shift)
        return s, v_ref[pl.ds(kj, ATT_TK), :]

    carry = _softmax_first(*scores(jnp.int32(0)))

    def body(off, carry):
        return _softmax_step(*scores(off), *carry)

    m, l, acc = lax.fori_loop(1, jnp.minimum(qi, DIL_NOFF - 1) + 1, body, carry)
    o_ref[...] = acc / l


def dilated_attention(qkv, slopes, table, batch, seq):
    nq = seq // ATT_TQ
    return pl.pallas_call(
        _dilated_kernel,
        grid=(batch, N_HEADS, nq),
        in_specs=[pl.BlockSpec(memory_space=pltpu.SMEM),
                  pl.BlockSpec((ATT_TQ, HEAD_DIM), lambda b, h, i: (b * nq + i, h)),
                  pl.BlockSpec((seq, HEAD_DIM), lambda b, h, i: (b, N_HEADS + h)),
                  pl.BlockSpec((seq, HEAD_DIM), lambda b, h, i: (b, 2 * N_HEADS + h)),
                  pl.BlockSpec((DIL_NOFF, ATT_TQ, ATT_TK), lambda b, h, i: (0, 0, 0))],
        out_specs=pl.BlockSpec((ATT_TQ, HEAD_DIM), lambda b, h, i: (b * nq + i, h)),
        out_shape=jax.ShapeDtypeStruct((batch * seq, WIDTH), F32),
        compiler_params=_params(3),
        name="dilated_attention",
    )(slopes, qkv, qkv, qkv, table)


def kernel(x, positions, ffn1_norm, ffn1_w_gate, ffn1_w_up, ffn1_w_down, mix_norm, w_in, g_cq, w_uq,
           g_ckv, w_ukv, g_out_a, g_out_b, w_o, ffn2_norm, ffn2_w_gate, ffn2_w_up, ffn2_w_down,
           final_norm):
    batch, seq, d = x.shape
    m = batch * seq
    depth = w_in.shape[0]

    inv_freq = 1.0 / (ROPE_THETA ** (jnp.arange(0, ROPE_DIM, 2, dtype=F32) / ROPE_DIM))
    ang = positions.astype(F32)[..., None] * inv_freq
    cos = jnp.cos(ang).reshape(m, ROPE_DIM // 2)
    sin = jnp.sin(ang).reshape(m, ROPE_DIM // 2)
    zeros = jnp.zeros((m, 128 - ROPE_DIM), F32)
    cos_p = jnp.concatenate([cos, cos, zeros], axis=-1)
    sin_p = jnp.concatenate([-sin, sin, zeros], axis=-1)

    slopes = jnp.asarray(
        np.array([2.0 ** (-8.0 * (i + 1) / N_HEADS) for i in range(N_HEADS)], dtype=np.float32))
    table = jnp.asarray(_dilated_log_mult_table())
    scale_a = HEAD_DIM ** -0.5
    scale_b = (HEAD_DIM + ROPE_DIM) ** -0.5

    h = x.reshape(m, d)
    for l in range(depth):
        n = rmsnorm(h, ffn1_norm[l], BF16)
        hid = ffn_up(n, ffn1_w_gate, ffn1_w_up, l)
        h = matmul(hid, ffn1_w_down, l, out_dtype=F32, residual=h, scale=0.5)

        n = rmsnorm(h, mix_norm[l], BF16)
        qkv_a = matmul(n, w_in, l, out_dtype=BF16, n_cols=3 * WIDTH, scale=scale_a,
                       n_scaled=WIDTH // 1024)
        w_c = jnp.concatenate(
            [w_in[l, :, 3 * WIDTH:], jnp.zeros((d, 128 - ROPE_DIM), F32)], axis=-1)
        cq_n, ckv_n, k_r = proj_c(n, w_c, g_cq[l], g_ckv[l], cos_p, sin_p)

        out_a = dilated_attention(qkv_a, slopes, table, batch, seq)

        wq = w_uq[l].reshape(Q_LORA, N_HEADS, HEAD_DIM + ROPE_DIM)
        wq_rope = jnp.pad(wq[:, :, HEAD_DIM:], ((0, 0), (0, 0), (0, 128 - ROPE_DIM)))
        w_q = jnp.concatenate([wq[:, :, :HEAD_DIM].reshape(Q_LORA, WIDTH),
                               wq_rope.reshape(Q_LORA, N_HEADS * 128)], axis=-1)
        q_all = q_up(cq_n, w_q, cos_p, sin_p, scale_b)
        kv = matmul(ckv_n, w_ukv, l, out_dtype=BF16)
        out_b = mla_attention(q_all, kv, k_r, batch, seq)

        mixed = outnorm(out_a, out_b, g_out_a[l], g_out_b[l])
        h = matmul(mixed, w_o, l, out_dtype=F32, residual=h, scale=1.0)

        n = rmsnorm(h, ffn2_norm[l], BF16)
        hid = ffn_up(n, ffn2_w_gate, ffn2_w_up, l)
        h = matmul(hid, ffn2_w_down, l, out_dtype=F32, residual=h, scale=0.5)

    return rmsnorm(h, final_norm, F32).reshape(batch, seq, d)
```

```python
import functools
import math

import numpy as np
import jax
import jax.numpy as jnp
from jax import lax
from jax.experimental import pallas as pl
from jax.experimental.pallas import tpu as pltpu

F32 = jnp.float32
BF16 = jnp.bfloat16

HEAD_DIM = 128
N_HEADS = 16
WIDTH = N_HEADS * HEAD_DIM
DIL_CONFIGS = ((128, 1), (512, 4), (2048, 16))
ROPE_DIM = 64
Q_LORA = 1024
KV_LORA = 512
ROPE_THETA = 10000.0
EPS = 1e-6
MASK = -1e30
LOG2E = math.log2(math.e)

VMEM_LIMIT = 56 * 1024 * 1024
ATT_TQ = 512
ATT_TK = 512
DIL_MAX_DIST = max(w for w, _ in DIL_CONFIGS)
DIL_NOFF = DIL_MAX_DIST // ATT_TK + 1


def _params(n_axes):
    return pltpu.CompilerParams(
        dimension_semantics=("arbitrary",) * n_axes, vmem_limit_bytes=VMEM_LIMIT)


def _rmsnorm_kernel(x_ref, g_ref, o_ref):
    x = x_ref[...]
    ms = jnp.mean(x * x, axis=-1, keepdims=True)
    o_ref[...] = ((x * lax.rsqrt(ms + EPS)) * g_ref[...]).astype(o_ref.dtype)


def rmsnorm(x, g, out_dtype, tm=256):
    m, d = x.shape
    return pl.pallas_call(
        _rmsnorm_kernel,
        grid=(m // tm,),
        in_specs=[pl.BlockSpec((tm, d), lambda i: (i, 0)),
                  pl.BlockSpec((1, d), lambda i: (0, 0))],
        out_specs=pl.BlockSpec((tm, d), lambda i: (i, 0)),
        out_shape=jax.ShapeDtypeStruct((m, d), out_dtype),
        compiler_params=_params(1),
        name="rmsnorm",
    )(x, g.reshape(1, d))


def _outnorm_kernel(a_ref, b_ref, ga_ref, gb_ref, o_ref):
    def norm(x, g):
        ms = jnp.mean(x * x, axis=-1, keepdims=True)
        return ((x * lax.rsqrt(ms + EPS)) * g).astype(o_ref.dtype)
    o_ref[:, :WIDTH] = norm(a_ref[...], ga_ref[...])
    o_ref[:, WIDTH:] = norm(b_ref[...], gb_ref[...])


def outnorm(out_a, out_b, g_a, g_b, tm=256):
    m = out_a.shape[0]
    return pl.pallas_call(
        _outnorm_kernel,
        grid=(m // tm,),
        in_specs=[pl.BlockSpec((tm, WIDTH), lambda i: (i, 0)),
                  pl.BlockSpec((tm, WIDTH), lambda i: (i, 0)),
                  pl.BlockSpec((1, WIDTH), lambda i: (0, 0)),
                  pl.BlockSpec((1, WIDTH), lambda i: (0, 0))],
        out_specs=pl.BlockSpec((tm, 2 * WIDTH), lambda i: (i, 0)),
        out_shape=jax.ShapeDtypeStruct((m, 2 * WIDTH), BF16),
        compiler_params=_params(1),
        name="outnorm",
    )(out_a, out_b, g_a.reshape(1, WIDTH), g_b.reshape(1, WIDTH))


def _dot(a, w):
    return jnp.dot(a, w.astype(BF16), preferred_element_type=F32)


def _w_spec(w, layer, block, index_map):
    if w.ndim == 3:
        return pl.BlockSpec((None,) + block, lambda *g: (layer,) + index_map(*g))
    return pl.BlockSpec(block, index_map)


def _mm_fullk_kernel(a_ref, w_ref, o_ref, *, scale, n_scaled):
    s = jnp.where(pl.program_id(1) < n_scaled, jnp.float32(scale), jnp.float32(1.0))
    o_ref[...] = (_dot(a_ref[...], w_ref[...]) * s).astype(o_ref.dtype)


def _mm_fullk_res_kernel(a_ref, w_ref, r_ref, o_ref, *, scale):
    o_ref[...] = r_ref[...] + scale * _dot(a_ref[...], w_ref[...])


def matmul_fullk(a, w, layer, *, out_dtype, n_cols=None, scale=1.0, n_scaled=0, residual=None,
                 tm=1024, tn=512):
    m, kdim = a.shape
    n = w.shape[-1] if n_cols is None else n_cols
    tn = min(tn, n)
    a_spec = pl.BlockSpec((tm, kdim), lambda i, j: (i, 0))
    w_spec = _w_spec(w, layer, (kdim, tn), lambda i, j: (0, j))
    o_spec = pl.BlockSpec((tm, tn), lambda i, j: (i, j))
    if residual is None:
        body = functools.partial(_mm_fullk_kernel, scale=scale, n_scaled=n_scaled)
        in_specs, args = [a_spec, w_spec], (a, w)
    else:
        body = functools.partial(_mm_fullk_res_kernel, scale=scale)
        in_specs, args = [a_spec, w_spec, o_spec], (a, w, residual)
    return pl.pallas_call(
        body,
        grid=(m // tm, n // tn),
        in_specs=in_specs,
        out_specs=o_spec,
        out_shape=jax.ShapeDtypeStruct((m, n), out_dtype),
        compiler_params=_params(2),
        name="matmul_fullk_res" if residual is not None else "matmul_fullk",
    )(*args)


def _ffn_up_kernel(a_ref, wg_ref, wu_ref, o_ref):
    a = a_ref[...]
    g = _dot(a, wg_ref[...])
    u = _dot(a, wu_ref[...])
    o_ref[...] = (g * jax.nn.sigmoid(g) * u).astype(o_ref.dtype)


def ffn_up(a, w_gate, w_up, layer, tm=1024, tn=256):
    m, kdim = a.shape
    n = w_gate.shape[-1]
    w_spec = _w_spec(w_gate, layer, (kdim, tn), lambda i, j: (0, j))
    return pl.pallas_call(
        _ffn_up_kernel,
        grid=(m // tm, n // tn),
        in_specs=[pl.BlockSpec((tm, kdim), lambda i, j: (i, 0)), w_spec, w_spec],
        out_specs=pl.BlockSpec((tm, tn), lambda i, j: (i, j)),
        out_shape=jax.ShapeDtypeStruct((m, n), BF16),
        compiler_params=_params(2),
        name="ffn_up",
    )(a, w_gate, w_up)


def _partial_dot(a_ref, w_ref, k_rem):
    a = a_ref[...]
    w = w_ref[...]
    if k_rem < a.shape[1]:
        col = lax.broadcasted_iota(jnp.int32, a.shape, 1)
        row = lax.broadcasted_iota(jnp.int32, w.shape, 0)
        a = jnp.where(col < k_rem, a, jnp.zeros_like(a))
        w = jnp.where(row < k_rem, w, jnp.zeros_like(w))
    return _dot(a, w)


def _mm_ksplit_res_kernel(a_ref, w_ref, r_ref, o_ref, acc_ref, *, nk, k_rem, scale):
    k = pl.program_id(2)

    @pl.when(k == 0)
    def _():
        acc_ref[...] = _dot(a_ref[...], w_ref[...])

    @pl.when(jnp.logical_and(k > 0, k < nk - 1))
    def _():
        acc_ref[...] += _dot(a_ref[...], w_ref[...])

    @pl.when(k == nk - 1)
    def _():
        o_ref[...] = r_ref[...] + scale * (acc_ref[...] + _partial_dot(a_ref, w_ref, k_rem))


def matmul_ksplit_res(a, w, layer, residual, scale, tm=1024, tn=1024, tk=1024):
    m, kdim = a.shape
    n = w.shape[-1]
    tn = min(tn, n)
    nk = pl.cdiv(kdim, tk)
    assert nk >= 2
    k_rem = kdim - (nk - 1) * tk
    o_spec = pl.BlockSpec((tm, tn), lambda j, i, k: (i, j))
    return pl.pallas_call(
        functools.partial(_mm_ksplit_res_kernel, nk=nk, k_rem=k_rem, scale=scale),
        grid=(n // tn, m // tm, nk),
        in_specs=[pl.BlockSpec((tm, tk), lambda j, i, k: (i, k)),
                  _w_spec(w, layer, (tk, tn), lambda j, i, k: (k, j)),
                  o_spec],
        out_specs=o_spec,
        out_shape=jax.ShapeDtypeStruct((m, n), F32),
        scratch_shapes=[pltpu.VMEM((tm, tn), F32)],
        compiler_params=_params(3),
        name="matmul_ksplit_res",
    )(a, w, residual)


def _rope_slab(x, cos_p, sin_p):
    lane = lax.broadcasted_iota(jnp.int32, x.shape, 1)
    half = ROPE_DIM // 2
    swapped = jnp.where(lane < half, pltpu.roll(x, 128 - half, 1), pltpu.roll(x, half, 1))
    return x * cos_p + swapped * sin_p


def _proj_c_kernel(a_ref, w_ref, gq_ref, gkv_ref, cos_ref, sin_ref,
                   cq_ref, ckv_ref, kr_ref, acc_ref, *, nk):
    k = pl.program_id(1)

    @pl.when(k == 0)
    def _():
        acc_ref[...] = _dot(a_ref[...], w_ref[...])

    @pl.when(k > 0)
    def _():
        acc_ref[...] += _dot(a_ref[...], w_ref[...])

    @pl.when(k == nk - 1)
    def _():
        def norm(x, g):
            ms = jnp.mean(x * x, axis=-1, keepdims=True)
            return (x * lax.rsqrt(ms + EPS)) * g
        cq_ref[...] = norm(acc_ref[:, :Q_LORA], gq_ref[...]).astype(cq_ref.dtype)
        ckv_ref[...] = norm(acc_ref[:, Q_LORA:Q_LORA + KV_LORA], gkv_ref[...]).astype(ckv_ref.dtype)
        kr = acc_ref[:, Q_LORA + KV_LORA:]
        kr_ref[...] = _rope_slab(kr, cos_ref[...], sin_ref[...]).astype(kr_ref.dtype)


def proj_c(a, w_c, g_cq, g_ckv, cos_p, sin_p, tm=512, tk=1024):
    m, kdim = a.shape
    n = w_c.shape[1]
    tk = min(tk, kdim)
    nk = kdim // tk
    row = lambda width: pl.BlockSpec((tm, width), lambda i, k: (i, 0))
    vec = lambda width: pl.BlockSpec((1, width), lambda i, k: (0, 0))
    return pl.pallas_call(
        functools.partial(_proj_c_kernel, nk=nk),
        grid=(m // tm, nk),
        in_specs=[pl.BlockSpec((tm, tk), lambda i, k: (i, k)),
                  pl.BlockSpec((tk, n), lambda i, k: (k, 0)),
                  vec(Q_LORA), vec(KV_LORA), row(128), row(128)],
        out_specs=[row(Q_LORA), row(KV_LORA), row(128)],
        out_shape=[jax.ShapeDtypeStruct((m, Q_LORA), BF16),
                   jax.ShapeDtypeStruct((m, KV_LORA), BF16),
                   jax.ShapeDtypeStruct((m, 128), BF16)],
        scratch_shapes=[pltpu.VMEM((tm, n), F32)],
        compiler_params=_params(2),
        name="proj_c",
    )(a, w_c, g_cq.reshape(1, Q_LORA), g_ckv.reshape(1, KV_LORA), cos_p, sin_p)


def _q_up_kernel(a_ref, w_ref, cos_ref, sin_ref, o_ref, *, scale, n_nope_tiles):
    x = _dot(a_ref[...], w_ref[...])
    j = pl.program_id(1)

    @pl.when(j < n_nope_tiles)
    def _():
        o_ref[...] = (x * scale).astype(o_ref.dtype)

    @pl.when(j >= n_nope_tiles)
    def _():
        cos_p = cos_ref[...]
        sin_p = sin_ref[...]
        for s in range(x.shape[1] // 128):
            slab = x[:, s * 128:(s + 1) * 128]
            o_ref[:, s * 128:(s + 1) * 128] = (_rope_slab(slab, cos_p, sin_p) * scale).astype(o_ref.dtype)


def q_up(a, w_q, cos_p, sin_p, scale, tm=1024, tn=1024):
    m, kdim = a.shape
    n = w_q.shape[1]
    return pl.pallas_call(
        functools.partial(_q_up_kernel, scale=scale, n_nope_tiles=WIDTH // tn),
        grid=(m // tm, n // tn),
        in_specs=[pl.BlockSpec((tm, kdim), lambda i, j: (i, 0)),
                  pl.BlockSpec((kdim, tn), lambda i, j: (0, j)),
                  pl.BlockSpec((tm, 128), lambda i, j: (i, 0)),
                  pl.BlockSpec((tm, 128), lambda i, j: (i, 0))],
        out_specs=pl.BlockSpec((tm, tn), lambda i, j: (i, j)),
        out_shape=jax.ShapeDtypeStruct((m, n), BF16),
        compiler_params=_params(2),
        name="q_up",
    )(a, w_q, cos_p, sin_p)


def _qk(q, k):
    return lax.dot_general(q, k, (((1,), (1,)), ((), ())), preferred_element_type=F32)


def _softmax_step(s, v, m, l, acc):
    m_new = jnp.maximum(m, jnp.max(s, axis=-1, keepdims=True))
    alpha = jnp.exp2(m - m_new)
    p = jnp.exp2(s - m_new)
    l = alpha * l + jnp.sum(p, axis=-1, keepdims=True)
    acc = alpha * acc + jnp.dot(p.astype(BF16), v, preferred_element_type=F32)
    return m_new, l, acc


def _softmax_first(s, v):
    m = jnp.max(s, axis=-1, keepdims=True)
    p = jnp.exp2(s - m)
    l = jnp.sum(p, axis=-1, keepdims=True)
    acc = jnp.dot(p.astype(BF16), v, preferred_element_type=F32)
    return m, l, acc


def _mla_kernel(qn_ref, qr_ref, kn_ref, kr_ref, v_ref, o_ref, kc_ref):
    qi = pl.program_id(2)

    @pl.when(qi == 0)
    def _():
        kc_ref[:, :HEAD_DIM] = kn_ref[...]
        kc_ref[:, HEAD_DIM:] = kr_ref[...]

    q = jnp.concatenate([qn_ref[...], qr_ref[...]], axis=-1)
    row = lax.broadcasted_iota(jnp.int32, (ATT_TQ, ATT_TK), 0)
    col = lax.broadcasted_iota(jnp.int32, (ATT_TQ, ATT_TK), 1)

    k0 = pl.multiple_of(qi * ATT_TK, ATT_TK)
    s = jnp.where(row >= col, _qk(q, kc_ref[pl.ds(k0, ATT_TK), :]), MASK)
    carry = _softmax_first(s, v_ref[pl.ds(k0, ATT_TK), :])

    def body(j, carry):
        kj = pl.multiple_of(j * ATT_TK, ATT_TK)
        s = _qk(q, kc_ref[pl.ds(kj, ATT_TK), :])
        return _softmax_step(s, v_ref[pl.ds(kj, ATT_TK), :], *carry)

    m, l, acc = lax.fori_loop(0, qi, body, carry)
    o_ref[...] = acc / l


def mla_attention(q_all, kv, k_r, batch, seq):
    nq = seq // ATT_TQ
    return pl.pallas_call(
        _mla_kernel,
        grid=(batch, N_HEADS, nq),
        in_specs=[pl.BlockSpec((ATT_TQ, HEAD_DIM), lambda b, h, i: (b * nq + i, h)),
                  pl.BlockSpec((ATT_TQ, HEAD_DIM), lambda b, h, i: (b * nq + i, N_HEADS + h)),
                  pl.BlockSpec((seq, HEAD_DIM), lambda b, h, i: (b, 2 * h)),
                  pl.BlockSpec((seq, HEAD_DIM), lambda b, h, i: (b, 0)),
                  pl.BlockSpec((seq, HEAD_DIM), lambda b, h, i: (b, 2 * h + 1))],
        out_specs=pl.BlockSpec((ATT_TQ, HEAD_DIM), lambda b, h, i: (b * nq + i, h)),
        out_shape=jax.ShapeDtypeStruct((batch * seq, WIDTH), F32),
        scratch_shapes=[pltpu.VMEM((seq, 2 * HEAD_DIM), BF16)],
        compiler_params=_params(3),
        name="mla_attention",
    )(q_all, q_all, kv, k_r, kv)


def _dilated_log2_mult_table():
    off = np.arange(DIL_NOFF)[:, None, None]
    i = np.arange(ATT_TQ)[None, :, None]
    j = np.arange(ATT_TK)[None, None, :]
    dist = off * ATT_TK + i - j
    mult = np.zeros(dist.shape, np.int32)
    for window, dilation in DIL_CONFIGS:
        mult += (dist >= 0) & (dist <= window) & (dist % dilation == 0)
    table = np.where(mult > 0, np.log2(np.maximum(mult, 1).astype(np.float64)), MASK)
    return table.astype(np.float32)


def _dilated_kernel(slope_ref, q_ref, k_ref, v_ref, t_ref, o_ref):
    h = pl.program_id(1)
    qi = pl.program_id(2)
    neg_slope = -slope_ref[h] * LOG2E
    q = q_ref[...]
    row = lax.broadcasted_iota(jnp.int32, (ATT_TQ, ATT_TK), 0)
    col = lax.broadcasted_iota(jnp.int32, (ATT_TQ, ATT_TK), 1)
    alibi0 = neg_slope * (row - col).astype(F32)

    def scores(off):
        kj = pl.multiple_of((qi - off) * ATT_TK, ATT_TK)
        shift = neg_slope * (off * ATT_TK).astype(F32)
        s = _qk(q, k_ref[pl.ds(kj, ATT_TK), :]) + alibi0 + (t_ref[off] + shift)
        return s, v_ref[pl.ds(kj, ATT_TK), :]

    carry = _softmax_first(*scores(jnp.int32(0)))

    def body(off, carry):
        return _softmax_step(*scores(off), *carry)

    m, l, acc = lax.fori_loop(1, jnp.minimum(qi, DIL_NOFF - 1) + 1, body, carry)
    o_ref[...] = acc / l


def dilated_attention(qkv, slopes, table, batch, seq):
    nq = seq // ATT_TQ
    return pl.pallas_call(
        _dilated_kernel,
        grid=(batch, N_HEADS, nq),
        in_specs=[pl.BlockSpec(memory_space=pltpu.SMEM),
                  pl.BlockSpec((ATT_TQ, HEAD_DIM), lambda b, h, i: (b * nq + i, h)),
                  pl.BlockSpec((seq, HEAD_DIM), lambda b, h, i: (b, N_HEADS + h)),
                  pl.BlockSpec((seq, HEAD_DIM), lambda b, h, i: (b, 2 * N_HEADS + h)),
                  pl.BlockSpec((DIL_NOFF, ATT_TQ, ATT_TK), lambda b, h, i: (0, 0, 0))],
        out_specs=pl.BlockSpec((ATT_TQ, HEAD_DIM), lambda b, h, i: (b * nq + i, h)),
        out_shape=jax.ShapeDtypeStruct((batch * seq, WIDTH), F32),
        compiler_params=_params(3),
        name="dilated_attention",
    )(slopes, qkv, qkv, qkv, table)


def kernel(x, positions, ffn1_norm, ffn1_w_gate, ffn1_w_up, ffn1_w_down, mix_norm, w_in, g_cq, w_uq,
           g_ckv, w_ukv, g_out_a, g_out_b, w_o, ffn2_norm, ffn2_w_gate, ffn2_w_up, ffn2_w_down,
           final_norm):
    batch, seq, d = x.shape
    m = batch * seq
    depth = w_in.shape[0]

    inv_freq = 1.0 / (ROPE_THETA ** (jnp.arange(0, ROPE_DIM, 2, dtype=F32) / ROPE_DIM))
    ang = positions.astype(F32)[..., None] * inv_freq
    cos = jnp.cos(ang).reshape(m, ROPE_DIM // 2)
    sin = jnp.sin(ang).reshape(m, ROPE_DIM // 2)
    zeros = jnp.zeros((m, 128 - ROPE_DIM), F32)
    cos_p = jnp.concatenate([cos, cos, zeros], axis=-1)
    sin_p = jnp.concatenate([-sin, sin, zeros], axis=-1)

    slopes = jnp.asarray(
        np.array([2.0 ** (-8.0 * (i + 1) / N_HEADS) for i in range(N_HEADS)], dtype=np.float32))
    table = jnp.asarray(_dilated_log2_mult_table())
    scale_a = HEAD_DIM ** -0.5 * LOG2E
    scale_b = (HEAD_DIM + ROPE_DIM) ** -0.5 * LOG2E

    h = x.reshape(m, d)
    for l in range(depth):
        n = rmsnorm(h, ffn1_norm[l], BF16)
        hid = ffn_up(n, ffn1_w_gate, ffn1_w_up, l)
        h = matmul_ksplit_res(hid, ffn1_w_down[l].astype(BF16), None, h, 0.5)

        n = rmsnorm(h, mix_norm[l], BF16)
        qkv_a = matmul_fullk(n, w_in, l, out_dtype=BF16, n_cols=3 * WIDTH, scale=scale_a,
                             n_scaled=WIDTH // 512)
        w_c = jnp.concatenate(
            [w_in[l, :, 3 * WIDTH:], jnp.zeros((d, 128 - ROPE_DIM), F32)], axis=-1)
        cq_n, ckv_n, k_r = proj_c(n, w_c, g_cq[l], g_ckv[l], cos_p, sin_p)

        out_a = dilated_attention(qkv_a, slopes, table, batch, seq)

        wq = w_uq[l].reshape(Q_LORA, N_HEADS, HEAD_DIM + ROPE_DIM)
        wq_rope = jnp.pad(wq[:, :, HEAD_DIM:], ((0, 0), (0, 0), (0, 128 - ROPE_DIM)))
        w_q = jnp.concatenate([wq[:, :, :HEAD_DIM].reshape(Q_LORA, WIDTH),
                               wq_rope.reshape(Q_LORA, N_HEADS * 128)], axis=-1)
        q_all = q_up(cq_n, w_q, cos_p, sin_p, scale_b)
        kv = matmul_fullk(ckv_n, w_ukv, l, out_dtype=BF16, tn=1024)
        out_b = mla_attention(q_all, kv, k_r, batch, seq)

        mixed = outnorm(out_a, out_b, g_out_a[l], g_out_b[l])
        h = matmul_fullk(mixed, w_o, l, out_dtype=F32, residual=h, scale=1.0)

        n = rmsnorm(h, ffn2_norm[l], BF16)
        hid = ffn_up(n, ffn2_w_gate, ffn2_w_up, l)
        h = matmul_ksplit_res(hid, ffn2_w_down[l].astype(BF16), None, h, 0.5)

    return rmsnorm(h, final_norm, F32).reshape(batch, seq, d)
```

```python
import functools
import math

import numpy as np
import jax
import jax.numpy as jnp
from jax import lax
from jax.experimental import pallas as pl
from jax.experimental.pallas import tpu as pltpu

F32 = jnp.float32
BF16 = jnp.bfloat16

HEAD_DIM = 128
N_HEADS = 16
WIDTH = N_HEADS * HEAD_DIM
DIL_CONFIGS = ((128, 1), (512, 4), (2048, 16))
ROPE_DIM = 64
Q_LORA = 1024
KV_LORA = 512
ROPE_THETA = 10000.0
EPS = 1e-6
MASK = -1e30
LOG2E = math.log2(math.e)

VMEM_LIMIT = 56 * 1024 * 1024
ATT_TQ = 512
ATT_TK = 512
DIL_MAX_DIST = max(w for w, _ in DIL_CONFIGS)
DIL_NOFF = DIL_MAX_DIST // ATT_TK + 1


def _params(n_axes):
    return pltpu.CompilerParams(
        dimension_semantics=("arbitrary",) * n_axes, vmem_limit_bytes=VMEM_LIMIT)


def _rmsnorm_kernel(x_ref, g_ref, o_ref):
    x = x_ref[...]
    ms = jnp.mean(x * x, axis=-1, keepdims=True)
    o_ref[...] = ((x * lax.rsqrt(ms + EPS)) * g_ref[...]).astype(o_ref.dtype)


def rmsnorm(x, g, out_dtype, tm=256):
    m, d = x.shape
    return pl.pallas_call(
        _rmsnorm_kernel,
        grid=(m // tm,),
        in_specs=[pl.BlockSpec((tm, d), lambda i: (i, 0)),
                  pl.BlockSpec((1, d), lambda i: (0, 0))],
        out_specs=pl.BlockSpec((tm, d), lambda i: (i, 0)),
        out_shape=jax.ShapeDtypeStruct((m, d), out_dtype),
        compiler_params=_params(1),
        name="rmsnorm",
    )(x, g.reshape(1, d))


def _outnorm_kernel(a_ref, b_ref, ga_ref, gb_ref, o_ref):
    def norm(x, g):
        ms = jnp.mean(x * x, axis=-1, keepdims=True)
        return ((x * lax.rsqrt(ms + EPS)) * g).astype(o_ref.dtype)
    o_ref[:, :WIDTH] = norm(a_ref[...], ga_ref[...])
    o_ref[:, WIDTH:] = norm(b_ref[...], gb_ref[...])


def outnorm(out_a, out_b, g_a, g_b, tm=256):
    m = out_a.shape[0]
    return pl.pallas_call(
        _outnorm_kernel,
        grid=(m // tm,),
        in_specs=[pl.BlockSpec((tm, WIDTH), lambda i: (i, 0)),
                  pl.BlockSpec((tm, WIDTH), lambda i: (i, 0)),
                  pl.BlockSpec((1, WIDTH), lambda i: (0, 0)),
                  pl.BlockSpec((1, WIDTH), lambda i: (0, 0))],
        out_specs=pl.BlockSpec((tm, 2 * WIDTH), lambda i: (i, 0)),
        out_shape=jax.ShapeDtypeStruct((m, 2 * WIDTH), BF16),
        compiler_params=_params(1),
        name="outnorm",
    )(out_a, out_b, g_a.reshape(1, WIDTH), g_b.reshape(1, WIDTH))


def _dot(a, w):
    return jnp.dot(a, w.astype(BF16), preferred_element_type=F32)


def _w_spec(w, layer, block, index_map):
    if w.ndim == 3:
        return pl.BlockSpec((None,) + block, lambda *g: (layer,) + index_map(*g))
    return pl.BlockSpec(block, index_map)


def _mm_fullk_kernel(a_ref, w_ref, o_ref, *, scale, n_scaled):
    s = jnp.where(pl.program_id(1) < n_scaled, jnp.float32(scale), jnp.float32(1.0))
    o_ref[...] = (_dot(a_ref[...], w_ref[...]) * s).astype(o_ref.dtype)


def _mm_fullk_res_kernel(a_ref, w_ref, r_ref, o_ref, *, scale):
    o_ref[...] = r_ref[...] + scale * _dot(a_ref[...], w_ref[...])


def matmul_fullk(a, w, layer, *, out_dtype, n_cols=None, scale=1.0, n_scaled=0, residual=None,
                 tm=1024, tn=512):
    m, kdim = a.shape
    n = w.shape[-1] if n_cols is None else n_cols
    tn = min(tn, n)
    a_spec = pl.BlockSpec((tm, kdim), lambda i, j: (i, 0))
    w_spec = _w_spec(w, layer, (kdim, tn), lambda i, j: (0, j))
    o_spec = pl.BlockSpec((tm, tn), lambda i, j: (i, j))
    if residual is None:
        body = functools.partial(_mm_fullk_kernel, scale=scale, n_scaled=n_scaled)
        in_specs, args = [a_spec, w_spec], (a, w)
    else:
        body = functools.partial(_mm_fullk_res_kernel, scale=scale)
        in_specs, args = [a_spec, w_spec, o_spec], (a, w, residual)
    return pl.pallas_call(
        body,
        grid=(m // tm, n // tn),
        in_specs=in_specs,
        out_specs=o_spec,
        out_shape=jax.ShapeDtypeStruct((m, n), out_dtype),
        compiler_params=_params(2),
        name="matmul_fullk_res" if residual is not None else "matmul_fullk",
    )(*args)


def _ffn_up_kernel(a_ref, wg_ref, wu_ref, o_ref):
    a = a_ref[...]
    g = _dot(a, wg_ref[...])
    u = _dot(a, wu_ref[...])
    o_ref[...] = (g * jax.nn.sigmoid(g) * u).astype(o_ref.dtype)


def ffn_up(a, w_gate, w_up, layer, tm=1024, tn=256):
    m, kdim = a.shape
    n = w_gate.shape[-1]
    w_spec = _w_spec(w_gate, layer, (kdim, tn), lambda i, j: (0, j))
    return pl.pallas_call(
        _ffn_up_kernel,
        grid=(m // tm, n // tn),
        in_specs=[pl.BlockSpec((tm, kdim), lambda i, j: (i, 0)), w_spec, w_spec],
        out_specs=pl.BlockSpec((tm, tn), lambda i, j: (i, j)),
        out_shape=jax.ShapeDtypeStruct((m, n), BF16),
        compiler_params=_params(2),
        name="ffn_up",
    )(a, w_gate, w_up)


def _mm_ksplit_res_kernel(a_ref, w_ref, r_ref, o_ref, acc_ref, *, nk, k_rem, scale):
    k = pl.program_id(2)

    @pl.when(k == 0)
    def _():
        acc_ref[...] = _dot(a_ref[...], w_ref[...])

    @pl.when(jnp.logical_and(k > 0, k < nk - 1))
    def _():
        acc_ref[...] += _dot(a_ref[...], w_ref[...])

    @pl.when(k == nk - 1)
    def _():
        tail = _dot(a_ref[:, :k_rem], w_ref[:k_rem, :])
        o_ref[...] = r_ref[...] + scale * (acc_ref[...] + tail)


def matmul_ksplit_res(a, w, layer, residual, scale, tm=1024, tn=1024, tk=2048):
    m, kdim = a.shape
    n = w.shape[-1]
    tn = min(tn, n)
    tk = min(tk, kdim)
    nk = pl.cdiv(kdim, tk)
    assert nk >= 2
    k_rem = kdim - (nk - 1) * tk
    o_spec = pl.BlockSpec((tm, tn), lambda j, i, k: (i, j))
    return pl.pallas_call(
        functools.partial(_mm_ksplit_res_kernel, nk=nk, k_rem=k_rem, scale=scale),
        grid=(n // tn, m // tm, nk),
        in_specs=[pl.BlockSpec((tm, tk), lambda j, i, k: (i, k)),
                  _w_spec(w, layer, (tk, tn), lambda j, i, k: (k, j)),
                  o_spec],
        out_specs=o_spec,
        out_shape=jax.ShapeDtypeStruct((m, n), F32),
        scratch_shapes=[pltpu.VMEM((tm, tn), F32)],
        compiler_params=_params(3),
        name="matmul_ksplit_res",
    )(a, w, residual)


def _rope_slab(x, cos_p, sin_p):
    lane = lax.broadcasted_iota(jnp.int32, x.shape, 1)
    half = ROPE_DIM // 2
    swapped = jnp.where(lane < half, pltpu.roll(x, 128 - half, 1), pltpu.roll(x, half, 1))
    return x * cos_p + swapped * sin_p


def _proj_c_kernel(a_ref, w_ref, gq_ref, gkv_ref, cos_ref, sin_ref,
                   cq_ref, ckv_ref, kr_ref, acc_ref, *, nk):
    k = pl.program_id(1)

    @pl.when(k == 0)
    def _():
        acc_ref[...] = _dot(a_ref[...], w_ref[...])

    @pl.when(k > 0)
    def _():
        acc_ref[...] += _dot(a_ref[...], w_ref[...])

    @pl.when(k == nk - 1)
    def _():
        def norm(x, g):
            ms = jnp.mean(x * x, axis=-1, keepdims=True)
            return (x * lax.rsqrt(ms + EPS)) * g
        cq_ref[...] = norm(acc_ref[:, :Q_LORA], gq_ref[...]).astype(cq_ref.dtype)
        ckv_ref[...] = norm(acc_ref[:, Q_LORA:Q_LORA + KV_LORA], gkv_ref[...]).astype(ckv_ref.dtype)
        kr = acc_ref[:, Q_LORA + KV_LORA:]
        kr_ref[...] = _rope_slab(kr, cos_ref[...], sin_ref[...]).astype(kr_ref.dtype)


def proj_c(a, w_c, g_cq, g_ckv, cos_p, sin_p, tm=512, tk=1024):
    m, kdim = a.shape
    n = w_c.shape[1]
    tk = min(tk, kdim)
    nk = kdim // tk
    row = lambda width: pl.BlockSpec((tm, width), lambda i, k: (i, 0))
    vec = lambda width: pl.BlockSpec((1, width), lambda i, k: (0, 0))
    return pl.pallas_call(
        functools.partial(_proj_c_kernel, nk=nk),
        grid=(m // tm, nk),
        in_specs=[pl.BlockSpec((tm, tk), lambda i, k: (i, k)),
                  pl.BlockSpec((tk, n), lambda i, k: (k, 0)),
                  vec(Q_LORA), vec(KV_LORA), row(128), row(128)],
        out_specs=[row(Q_LORA), row(KV_LORA), row(128)],
        out_shape=[jax.ShapeDtypeStruct((m, Q_LORA), BF16),
                   jax.ShapeDtypeStruct((m, KV_LORA), BF16),
                   jax.ShapeDtypeStruct((m, 128), BF16)],
        scratch_shapes=[pltpu.VMEM((tm, n), F32)],
        compiler_params=_params(2),
        name="proj_c",
    )(a, w_c, g_cq.reshape(1, Q_LORA), g_ckv.reshape(1, KV_LORA), cos_p, sin_p)


def _q_up_kernel(a_ref, w_ref, cos_ref, sin_ref, o_ref, *, scale, n_nope_tiles):
    x = _dot(a_ref[...], w_ref[...])
    j = pl.program_id(1)

    @pl.when(j < n_nope_tiles)
    def _():
        o_ref[...] = (x * scale).astype(o_ref.dtype)

    @pl.when(j >= n_nope_tiles)
    def _():
        cos_p = cos_ref[...]
        sin_p = sin_ref[...]
        for s in range(x.shape[1] // 128):
            slab = x[:, s * 128:(s + 1) * 128]
            o_ref[:, s * 128:(s + 1) * 128] = (_rope_slab(slab, cos_p, sin_p) * scale).astype(o_ref.dtype)


def q_up(a, w_q, cos_p, sin_p, scale, tm=1024, tn=1024):
    m, kdim = a.shape
    n = w_q.shape[1]
    return pl.pallas_call(
        functools.partial(_q_up_kernel, scale=scale, n_nope_tiles=WIDTH // tn),
        grid=(m // tm, n // tn),
        in_specs=[pl.BlockSpec((tm, kdim), lambda i, j: (i, 0)),
                  pl.BlockSpec((kdim, tn), lambda i, j: (0, j)),
                  pl.BlockSpec((tm, 128), lambda i, j: (i, 0)),
                  pl.BlockSpec((tm, 128), lambda i, j: (i, 0))],
        out_specs=pl.BlockSpec((tm, tn), lambda i, j: (i, j)),
        out_shape=jax.ShapeDtypeStruct((m, n), BF16),
        compiler_params=_params(2),
        name="q_up",
    )(a, w_q, cos_p, sin_p)


def _qk(q, k):
    return lax.dot_general(q, k, (((1,), (1,)), ((), ())), preferred_element_type=F32)


def _softmax_update(s, v, m_ref, l_ref, acc_ref):
    m_prev = m_ref[...]
    m_new = jnp.maximum(m_prev, jnp.max(s, axis=-1, keepdims=True))
    alpha = jnp.exp2(m_prev - m_new)
    p = jnp.exp2(s - pltpu.repeat(m_new, s.shape[1] // 128, axis=1))
    l_ref[...] = alpha * l_ref[...] + jnp.sum(p, axis=-1, keepdims=True)
    acc_ref[...] = alpha * acc_ref[...] + jnp.dot(p.astype(BF16), v, preferred_element_type=F32)
    m_ref[...] = m_new


def _softmax_reset(m_ref, l_ref, acc_ref):
    m_ref[...] = jnp.full(m_ref.shape, MASK, F32)
    l_ref[...] = jnp.zeros(l_ref.shape, F32)
    acc_ref[...] = jnp.zeros(acc_ref.shape, F32)


def _softmax_scratch():
    return [pltpu.VMEM((2, ATT_TQ, ATT_TK), F32), pltpu.VMEM((ATT_TQ, 128), F32),
            pltpu.VMEM((ATT_TQ, 128), F32), pltpu.VMEM((ATT_TQ, HEAD_DIM), F32)]


def _mla_kernel(qn_ref, qr_ref, kn_ref, kr_ref, v_ref, o_ref, kc_ref, s_ref, m_ref, l_ref, acc_ref):
    qi = pl.program_id(2)

    @pl.when(qi == 0)
    def _():
        kc_ref[:, :HEAD_DIM] = kn_ref[...]
        kc_ref[:, HEAD_DIM:] = kr_ref[...]

    q = jnp.concatenate([qn_ref[...], qr_ref[...]], axis=-1)
    state = (m_ref, l_ref, acc_ref)
    _softmax_reset(*state)

    def scores(j):
        return _qk(q, kc_ref[pl.ds(pl.multiple_of(j * ATT_TK, ATT_TK), ATT_TK), :])

    def values(j):
        return v_ref[pl.ds(pl.multiple_of(j * ATT_TK, ATT_TK), ATT_TK), :]

    s_ref[0] = scores(jnp.int32(0))

    def pair(jj, _):
        j = 2 * jj
        s_ref[1] = scores(j + 1)
        _softmax_update(s_ref[0], values(j), *state)
        s_ref[0] = scores(j + 2)
        _softmax_update(s_ref[1], values(j + 1), *state)
        return 0

    lax.fori_loop(0, qi // 2, pair, 0)

    @pl.when(qi % 2 == 1)
    def _():
        s_ref[1] = scores(qi)
        _softmax_update(s_ref[0], values(qi - 1), *state)

    row = lax.broadcasted_iota(jnp.int32, (ATT_TQ, ATT_TK), 0)
    col = lax.broadcasted_iota(jnp.int32, (ATT_TQ, ATT_TK), 1)
    _softmax_update(jnp.where(row >= col, s_ref[qi % 2], MASK), values(qi), *state)
    o_ref[...] = acc_ref[...] / l_ref[...]


def mla_attention(q_all, kv, k_r, batch, seq):
    nq = seq // ATT_TQ
    return pl.pallas_call(
        _mla_kernel,
        grid=(batch, N_HEADS, nq),
        in_specs=[pl.BlockSpec((ATT_TQ, HEAD_DIM), lambda b, h, i: (b * nq + i, h)),
                  pl.BlockSpec((ATT_TQ, HEAD_DIM), lambda b, h, i: (b * nq + i, N_HEADS + h)),
                  pl.BlockSpec((seq, HEAD_DIM), lambda b, h, i: (b, 2 * h)),
                  pl.BlockSpec((seq, HEAD_DIM), lambda b, h, i: (b, 0)),
                  pl.BlockSpec((seq, HEAD_DIM), lambda b, h, i: (b, 2 * h + 1))],
        out_specs=pl.BlockSpec((ATT_TQ, HEAD_DIM), lambda b, h, i: (b * nq + i, h)),
        out_shape=jax.ShapeDtypeStruct((batch * seq, WIDTH), F32),
        scratch_shapes=[pltpu.VMEM((seq, 2 * HEAD_DIM), BF16)] + _softmax_scratch(),
        compiler_params=_params(3),
        name="mla_attention",
    )(q_all, q_all, kv, k_r, kv)


def _dilated_bias(off, neg_slope):
    row = lax.broadcasted_iota(jnp.int32, (ATT_TQ, ATT_TK), 0)
    col = lax.broadcasted_iota(jnp.int32, (ATT_TQ, ATT_TK), 1)
    dist = row - col + off * ATT_TK
    mult = jnp.zeros((ATT_TQ, ATT_TK), jnp.int32)
    for window, dilation in DIL_CONFIGS:
        assert dilation & (dilation - 1) == 0
        held = (dist >= 0) & (dist <= window) & ((dist & (dilation - 1)) == 0)
        mult = mult + held.astype(jnp.int32)
    log2_mult = jnp.full((ATT_TQ, ATT_TK), MASK, F32)
    for n in range(1, len(DIL_CONFIGS) + 1):
        log2_mult = jnp.where(mult == n, jnp.float32(math.log2(n)), log2_mult)
    return neg_slope * dist.astype(F32) + log2_mult


def _dilated_kernel(slope_ref, q_ref, k_ref, v_ref, o_ref, bias_ref, s_ref, m_ref, l_ref, acc_ref):
    h = pl.program_id(1)
    qi = pl.program_id(2)

    @pl.when(qi == 0)
    def _():
        neg_slope = -slope_ref[h] * LOG2E
        for off in range(DIL_NOFF):
            bias_ref[off] = _dilated_bias(off, neg_slope)

    q = q_ref[...]
    state = (m_ref, l_ref, acc_ref)
    _softmax_reset(*state)

    def scores(off):
        kj = pl.multiple_of((qi - off) * ATT_TK, ATT_TK)
        return _qk(q, k_ref[pl.ds(kj, ATT_TK), :]) + bias_ref[off]

    def values(off):
        return v_ref[pl.ds(pl.multiple_of((qi - off) * ATT_TK, ATT_TK), ATT_TK), :]

    n_far = jnp.minimum(qi, DIL_NOFF - 1)
    s_ref[0] = scores(n_far)

    def pair(tt, _):
        off = n_far - 2 * tt
        s_ref[1] = scores(off - 1)
        _softmax_update(s_ref[0], values(off), *state)
        s_ref[0] = scores(off - 2)
        _softmax_update(s_ref[1], values(off - 1), *state)
        return 0

    lax.fori_loop(0, n_far // 2, pair, 0)

    @pl.when(n_far % 2 == 1)
    def _():
        s_ref[1] = scores(jnp.int32(0))
        _softmax_update(s_ref[0], values(jnp.int32(1)), *state)

    _softmax_update(s_ref[n_far % 2], values(jnp.int32(0)), *state)
    o_ref[...] = acc_ref[...] / l_ref[...]


def dilated_attention(qkv, slopes, batch, seq):
    nq = seq // ATT_TQ
    return pl.pallas_call(
        _dilated_kernel,
        grid=(batch, N_HEADS, nq),
        in_specs=[pl.BlockSpec(memory_space=pltpu.SMEM),
                  pl.BlockSpec((ATT_TQ, HEAD_DIM), lambda b, h, i: (b * nq + i, h)),
                  pl.BlockSpec((seq, HEAD_DIM), lambda b, h, i: (b, N_HEADS + h)),
                  pl.BlockSpec((seq, HEAD_DIM), lambda b, h, i: (b, 2 * N_HEADS + h))],
        out_specs=pl.BlockSpec((ATT_TQ, HEAD_DIM), lambda b, h, i: (b * nq + i, h)),
        out_shape=jax.ShapeDtypeStruct((batch * seq, WIDTH), F32),
        scratch_shapes=[pltpu.VMEM((DIL_NOFF, ATT_TQ, ATT_TK), F32)] + _softmax_scratch(),
        compiler_params=_params(3),
        name="dilated_attention",
    )(slopes, qkv, qkv, qkv)


def kernel(x, positions, ffn1_norm, ffn1_w_gate, ffn1_w_up, ffn1_w_down, mix_norm, w_in, g_cq, w_uq,
           g_ckv, w_ukv, g_out_a, g_out_b, w_o, ffn2_norm, ffn2_w_gate, ffn2_w_up, ffn2_w_down,
           final_norm):
    batch, seq, d = x.shape
    m = batch * seq
    depth = w_in.shape[0]

    inv_freq = 1.0 / (ROPE_THETA ** (jnp.arange(0, ROPE_DIM, 2, dtype=F32) / ROPE_DIM))
    ang = positions.astype(F32)[..., None] * inv_freq
    cos = jnp.cos(ang).reshape(m, ROPE_DIM // 2)
    sin = jnp.sin(ang).reshape(m, ROPE_DIM // 2)
    zeros = jnp.zeros((m, 128 - ROPE_DIM), F32)
    cos_p = jnp.concatenate([cos, cos, zeros], axis=-1)
    sin_p = jnp.concatenate([-sin, sin, zeros], axis=-1)

    slopes = jnp.asarray(
        np.array([2.0 ** (-8.0 * (i + 1) / N_HEADS) for i in range(N_HEADS)], dtype=np.float32))
    scale_a = HEAD_DIM ** -0.5 * LOG2E
    scale_b = (HEAD_DIM + ROPE_DIM) ** -0.5 * LOG2E

    w_down1 = ffn1_w_down.astype(BF16)
    w_down2 = ffn2_w_down.astype(BF16)

    h = x.reshape(m, d)
    for l in range(depth):
        n = rmsnorm(h, ffn1_norm[l], BF16)
        hid = ffn_up(n, ffn1_w_gate, ffn1_w_up, l)
        h = matmul_ksplit_res(hid, w_down1, l, h, 0.5)

        n = rmsnorm(h, mix_norm[l], BF16)
        qkv_a = matmul_fullk(n, w_in, l, out_dtype=BF16, n_cols=3 * WIDTH, scale=scale_a,
                             n_scaled=WIDTH // 512)
        w_c = jnp.concatenate(
            [w_in[l, :, 3 * WIDTH:], jnp.zeros((d, 128 - ROPE_DIM), F32)], axis=-1)
        cq_n, ckv_n, k_r = proj_c(n, w_c, g_cq[l], g_ckv[l], cos_p, sin_p)

        out_a = dilated_attention(qkv_a, slopes, batch, seq)

        wq = w_uq[l].reshape(Q_LORA, N_HEADS, HEAD_DIM + ROPE_DIM)
        wq_rope = jnp.pad(wq[:, :, HEAD_DIM:], ((0, 0), (0, 0), (0, 128 - ROPE_DIM)))
        w_q = jnp.concatenate([wq[:, :, :HEAD_DIM].reshape(Q_LORA, WIDTH),
                               wq_rope.reshape(Q_LORA, N_HEADS * 128)], axis=-1)
        q_all = q_up(cq_n, w_q, cos_p, sin_p, scale_b)
        kv = matmul_fullk(ckv_n, w_ukv, l, out_dtype=BF16, tn=1024)
        out_b = mla_attention(q_all, kv, k_r, batch, seq)

        mixed = outnorm(out_a, out_b, g_out_a[l], g_out_b[l])
        h = matmul_fullk(mixed, w_o, l, out_dtype=F32, residual=h, scale=1.0)

        n = rmsnorm(h, ffn2_norm[l], BF16)
        hid = ffn_up(n, ffn2_w_gate, ffn2_w_up, l)
        h = matmul_ksplit_res(hid, w_down2, l, h, 0.5)

    return rmsnorm(h, final_norm, F32).reshape(batch, seq, d)
```

```python
import functools
import math

import numpy as np
import jax
import jax.numpy as jnp
from jax import lax
from jax.experimental import pallas as pl
from jax.experimental.pallas import tpu as pltpu

F32 = jnp.float32
BF16 = jnp.bfloat16

HEAD_DIM = 128
N_HEADS = 16
WIDTH = N_HEADS * HEAD_DIM
DIL_CONFIGS = ((128, 1), (512, 4), (2048, 16))
ROPE_DIM = 64
Q_LORA = 1024
KV_LORA = 512
ROPE_THETA = 10000.0
EPS = 1e-6
MASK = -1e30
LOG2E = math.log2(math.e)

VMEM_LIMIT = 56 * 1024 * 1024
ATT_TQ = 512
ATT_TK = 512
DIL_MAX_DIST = max(w for w, _ in DIL_CONFIGS)
DIL_NOFF = DIL_MAX_DIST // ATT_TK + 1


def _params(n_axes):
    return pltpu.CompilerParams(
        dimension_semantics=("arbitrary",) * n_axes, vmem_limit_bytes=VMEM_LIMIT)


def _rmsnorm_kernel(x_ref, g_ref, o_ref):
    x = x_ref[...]
    ms = jnp.mean(x * x, axis=-1, keepdims=True)
    o_ref[...] = ((x * lax.rsqrt(ms + EPS)) * g_ref[...]).astype(o_ref.dtype)


def rmsnorm(x, g, out_dtype, tm=256):
    m, d = x.shape
    return pl.pallas_call(
        _rmsnorm_kernel,
        grid=(m // tm,),
        in_specs=[pl.BlockSpec((tm, d), lambda i: (i, 0)),
                  pl.BlockSpec((1, d), lambda i: (0, 0))],
        out_specs=pl.BlockSpec((tm, d), lambda i: (i, 0)),
        out_shape=jax.ShapeDtypeStruct((m, d), out_dtype),
        compiler_params=_params(1),
        name="rmsnorm",
    )(x, g.reshape(1, d))


def _outnorm_kernel(a_ref, b_ref, ga_ref, gb_ref, o_ref):
    def norm(x, g):
        ms = jnp.mean(x * x, axis=-1, keepdims=True)
        return ((x * lax.rsqrt(ms + EPS)) * g).astype(o_ref.dtype)
    o_ref[:, :WIDTH] = norm(a_ref[...], ga_ref[...])
    o_ref[:, WIDTH:] = norm(b_ref[...], gb_ref[...])


def outnorm(out_a, out_b, g_a, g_b, tm=256):
    m = out_a.shape[0]
    return pl.pallas_call(
        _outnorm_kernel,
        grid=(m // tm,),
        in_specs=[pl.BlockSpec((tm, WIDTH), lambda i: (i, 0)),
                  pl.BlockSpec((tm, WIDTH), lambda i: (i, 0)),
                  pl.BlockSpec((1, WIDTH), lambda i: (0, 0)),
                  pl.BlockSpec((1, WIDTH), lambda i: (0, 0))],
        out_specs=pl.BlockSpec((tm, 2 * WIDTH), lambda i: (i, 0)),
        out_shape=jax.ShapeDtypeStruct((m, 2 * WIDTH), BF16),
        compiler_params=_params(1),
        name="outnorm",
    )(out_a, out_b, g_a.reshape(1, WIDTH), g_b.reshape(1, WIDTH))


def _dot(a, w):
    return jnp.dot(a, w.astype(BF16), preferred_element_type=F32)


def _dot_t(a, w_t):
    return lax.dot_general(a, w_t.astype(BF16), (((1,), (1,)), ((), ())), preferred_element_type=F32)


def _w_spec(w, layer, block, index_map):
    if w.ndim == 3:
        return pl.BlockSpec((None,) + block, lambda *g: (layer,) + index_map(*g))
    return pl.BlockSpec(block, index_map)


def _mm_fullk_kernel(a_ref, w_ref, o_ref, *, scale, n_scaled):
    s = jnp.where(pl.program_id(1) < n_scaled, jnp.float32(scale), jnp.float32(1.0))
    o_ref[...] = (_dot(a_ref[...], w_ref[...]) * s).astype(o_ref.dtype)


def _mm_fullk_res_kernel(a_ref, w_ref, r_ref, o_ref, *, scale):
    o_ref[...] = r_ref[...] + scale * _dot(a_ref[...], w_ref[...])


def _mm_fullk_t_kernel(a_ref, w_ref, o_ref, *, scale, n_scaled):
    s = jnp.where(pl.program_id(1) < n_scaled, jnp.float32(scale), jnp.float32(1.0))
    o_ref[...] = (_dot_t(a_ref[...], w_ref[...]) * s).astype(o_ref.dtype)


def matmul_fullk_t(a, w_t, layer, *, out_dtype, n_cols, scale=1.0, n_scaled=0, tm=1024, tn=512):
    m, kdim = a.shape
    return pl.pallas_call(
        functools.partial(_mm_fullk_t_kernel, scale=scale, n_scaled=n_scaled),
        grid=(m // tm, n_cols // tn),
        in_specs=[pl.BlockSpec((tm, kdim), lambda i, j: (i, 0)),
                  pl.BlockSpec((None, tn, kdim), lambda i, j: (layer, j, 0))],
        out_specs=pl.BlockSpec((tm, tn), lambda i, j: (i, j)),
        out_shape=jax.ShapeDtypeStruct((m, n_cols), out_dtype),
        compiler_params=_params(2),
        name="matmul_fullk_t",
    )(a, w_t)


def matmul_fullk(a, w, layer, *, out_dtype, n_cols=None, scale=1.0, n_scaled=0, residual=None,
                 tm=1024, tn=512):
    m, kdim = a.shape
    n = w.shape[-1] if n_cols is None else n_cols
    tn = min(tn, n)
    a_spec = pl.BlockSpec((tm, kdim), lambda i, j: (i, 0))
    w_spec = _w_spec(w, layer, (kdim, tn), lambda i, j: (0, j))
    o_spec = pl.BlockSpec((tm, tn), lambda i, j: (i, j))
    if residual is None:
        body = functools.partial(_mm_fullk_kernel, scale=scale, n_scaled=n_scaled)
        in_specs, args = [a_spec, w_spec], (a, w)
    else:
        body = functools.partial(_mm_fullk_res_kernel, scale=scale)
        in_specs, args = [a_spec, w_spec, o_spec], (a, w, residual)
    return pl.pallas_call(
        body,
        grid=(m // tm, n // tn),
        in_specs=in_specs,
        out_specs=o_spec,
        out_shape=jax.ShapeDtypeStruct((m, n), out_dtype),
        compiler_params=_params(2),
        name="matmul_fullk_res" if residual is not None else "matmul_fullk",
    )(*args)


def _ffn_up_kernel(a_ref, wg_ref, wu_ref, o_ref):
    a = a_ref[...]
    g = _dot(a, wg_ref[...])
    u = _dot(a, wu_ref[...])
    o_ref[...] = (g * jax.nn.sigmoid(g) * u).astype(o_ref.dtype)


def ffn_up(a, w_gate, w_up, layer, tm=1024, tn=256):
    m, kdim = a.shape
    n = w_gate.shape[-1]
    w_spec = _w_spec(w_gate, layer, (kdim, tn), lambda i, j: (0, j))
    return pl.pallas_call(
        _ffn_up_kernel,
        grid=(m // tm, n // tn),
        in_specs=[pl.BlockSpec((tm, kdim), lambda i, j: (i, 0)), w_spec, w_spec],
        out_specs=pl.BlockSpec((tm, tn), lambda i, j: (i, j)),
        out_shape=jax.ShapeDtypeStruct((m, n), BF16),
        compiler_params=_params(2),
        name="ffn_up",
    )(a, w_gate, w_up)


def _mm_ksplit_res_kernel(a_ref, w_ref, r_ref, o_ref, acc_ref, *, nk, k_rem, scale):
    k = pl.program_id(2)

    @pl.when(k == 0)
    def _():
        acc_ref[...] = _dot(a_ref[...], w_ref[...])

    @pl.when(jnp.logical_and(k > 0, k < nk - 1))
    def _():
        acc_ref[...] += _dot(a_ref[...], w_ref[...])

    @pl.when(k == nk - 1)
    def _():
        tail = _dot(a_ref[:, :k_rem], w_ref[:k_rem, :])
        o_ref[...] = r_ref[...] + scale * (acc_ref[...] + tail)


def matmul_ksplit_res(a, w, layer, residual, scale, tm=1024, tn=1024, tk=2816):
    m, kdim = a.shape
    n = w.shape[-1]
    tn = min(tn, n)
    tk = min(tk, kdim)
    nk = pl.cdiv(kdim, tk)
    assert nk >= 2
    k_rem = kdim - (nk - 1) * tk
    o_spec = pl.BlockSpec((tm, tn), lambda j, i, k: (i, j))
    return pl.pallas_call(
        functools.partial(_mm_ksplit_res_kernel, nk=nk, k_rem=k_rem, scale=scale),
        grid=(n // tn, m // tm, nk),
        in_specs=[pl.BlockSpec((tm, tk), lambda j, i, k: (i, k)),
                  _w_spec(w, layer, (tk, tn), lambda j, i, k: (k, j)),
                  o_spec],
        out_specs=o_spec,
        out_shape=jax.ShapeDtypeStruct((m, n), F32),
        scratch_shapes=[pltpu.VMEM((tm, tn), F32)],
        compiler_params=_params(3),
        name="matmul_ksplit_res",
    )(a, w, residual)


def _rope_slab(x, cos_p, sin_p):
    lane = lax.broadcasted_iota(jnp.int32, x.shape, 1)
    half = ROPE_DIM // 2
    swapped = jnp.where(lane < half, pltpu.roll(x, 128 - half, 1), pltpu.roll(x, half, 1))
    return x * cos_p + swapped * sin_p


def _proj_c_kernel(a_ref, w_ref, gq_ref, gkv_ref, cos_ref, sin_ref,
                   cq_ref, ckv_ref, kr_ref, acc_ref, *, nk):
    k = pl.program_id(1)

    @pl.when(k == 0)
    def _():
        acc_ref[...] = _dot_t(a_ref[...], w_ref[...])

    @pl.when(k > 0)
    def _():
        acc_ref[...] += _dot_t(a_ref[...], w_ref[...])

    @pl.when(k == nk - 1)
    def _():
        def norm(x, g):
            ms = jnp.mean(x * x, axis=-1, keepdims=True)
            return (x * lax.rsqrt(ms + EPS)) * g
        cq_ref[...] = norm(acc_ref[:, :Q_LORA], gq_ref[...]).astype(cq_ref.dtype)
        ckv_ref[...] = norm(acc_ref[:, Q_LORA:Q_LORA + KV_LORA], gkv_ref[...]).astype(ckv_ref.dtype)
        kr = acc_ref[:, Q_LORA + KV_LORA:]
        kr_ref[...] = _rope_slab(kr, cos_ref[...], sin_ref[...]).astype(kr_ref.dtype)


def proj_c(a, w_c_t, g_cq, g_ckv, cos_p, sin_p, tm=512, tk=1024):
    m, kdim = a.shape
    n = w_c_t.shape[0]
    tk = min(tk, kdim)
    nk = kdim // tk
    row = lambda width: pl.BlockSpec((tm, width), lambda i, k: (i, 0))
    vec = lambda width: pl.BlockSpec((1, width), lambda i, k: (0, 0))
    return pl.pallas_call(
        functools.partial(_proj_c_kernel, nk=nk),
        grid=(m // tm, nk),
        in_specs=[pl.BlockSpec((tm, tk), lambda i, k: (i, k)),
                  pl.BlockSpec((n, tk), lambda i, k: (0, k)),
                  vec(Q_LORA), vec(KV_LORA), row(128), row(128)],
        out_specs=[row(Q_LORA), row(KV_LORA), row(128)],
        out_shape=[jax.ShapeDtypeStruct((m, Q_LORA), BF16),
                   jax.ShapeDtypeStruct((m, KV_LORA), BF16),
                   jax.ShapeDtypeStruct((m, 128), BF16)],
        scratch_shapes=[pltpu.VMEM((tm, n), F32)],
        compiler_params=_params(2),
        name="proj_c",
    )(a, w_c_t, g_cq.reshape(1, Q_LORA), g_ckv.reshape(1, KV_LORA), cos_p, sin_p)


def _q_up_kernel(a_ref, w_ref, cos_ref, sin_ref, o_ref, *, scale, n_nope_tiles):
    x = _dot(a_ref[...], w_ref[...])
    j = pl.program_id(1)

    @pl.when(j < n_nope_tiles)
    def _():
        o_ref[...] = (x * scale).astype(o_ref.dtype)

    @pl.when(j >= n_nope_tiles)
    def _():
        cos_p = cos_ref[...]
        sin_p = sin_ref[...]
        for s in range(x.shape[1] // 128):
            slab = x[:, s * 128:(s + 1) * 128]
            o_ref[:, s * 128:(s + 1) * 128] = (_rope_slab(slab, cos_p, sin_p) * scale).astype(o_ref.dtype)


def q_up(a, w_q, cos_p, sin_p, scale, tm=1024, tn=1024):
    m, kdim = a.shape
    n = w_q.shape[1]
    return pl.pallas_call(
        functools.partial(_q_up_kernel, scale=scale, n_nope_tiles=WIDTH // tn),
        grid=(m // tm, n // tn),
        in_specs=[pl.BlockSpec((tm, kdim), lambda i, j: (i, 0)),
                  pl.BlockSpec((kdim, tn), lambda i, j: (0, j)),
                  pl.BlockSpec((tm, 128), lambda i, j: (i, 0)),
                  pl.BlockSpec((tm, 128), lambda i, j: (i, 0))],
        out_specs=pl.BlockSpec((tm, tn), lambda i, j: (i, j)),
        out_shape=jax.ShapeDtypeStruct((m, n), BF16),
        compiler_params=_params(2),
        name="q_up",
    )(a, w_q, cos_p, sin_p)


def _qk(q, k):
    return lax.dot_general(q, k, (((1,), (1,)), ((), ())), preferred_element_type=F32)


def _softmax_update(s, v, m_ref, l_ref, acc_ref):
    m_prev = m_ref[...]
    m_new = jnp.maximum(m_prev, jnp.max(s, axis=-1, keepdims=True))
    alpha = jnp.exp2(m_prev - m_new)
    p = jnp.exp2(s - pltpu.repeat(m_new, s.shape[1] // 128, axis=1))
    l_ref[...] = alpha * l_ref[...] + jnp.sum(p, axis=-1, keepdims=True)
    acc_ref[...] = alpha * acc_ref[...] + jnp.dot(p.astype(BF16), v, preferred_element_type=F32)
    m_ref[...] = m_new


def _softmax_reset(m_ref, l_ref, acc_ref):
    m_ref[...] = jnp.full(m_ref.shape, MASK, F32)
    l_ref[...] = jnp.zeros(l_ref.shape, F32)
    acc_ref[...] = jnp.zeros(acc_ref.shape, F32)


def _softmax_scratch():
    return [pltpu.VMEM((2, ATT_TQ, ATT_TK), F32), pltpu.VMEM((ATT_TQ, 128), F32),
            pltpu.VMEM((ATT_TQ, 128), F32), pltpu.VMEM((ATT_TQ, HEAD_DIM), F32)]


def _mla_kernel(qn_ref, qr_ref, kn_ref, kr_ref, v_ref, o_ref, kc_ref, s_ref, m_ref, l_ref, acc_ref):
    qi = pl.program_id(2)

    @pl.when(qi == 0)
    def _():
        kc_ref[:, :HEAD_DIM] = kn_ref[...]
        kc_ref[:, HEAD_DIM:] = kr_ref[...]

    q = jnp.concatenate([qn_ref[...], qr_ref[...]], axis=-1)
    state = (m_ref, l_ref, acc_ref)
    _softmax_reset(*state)

    def scores(j):
        return _qk(q, kc_ref[pl.ds(pl.multiple_of(j * ATT_TK, ATT_TK), ATT_TK), :])

    def values(j):
        return v_ref[pl.ds(pl.multiple_of(j * ATT_TK, ATT_TK), ATT_TK), :]

    s_ref[0] = scores(jnp.int32(0))

    def pair(jj, _):
        j = 2 * jj
        s_ref[1] = scores(j + 1)
        _softmax_update(s_ref[0], values(j), *state)
        s_ref[0] = scores(j + 2)
        _softmax_update(s_ref[1], values(j + 1), *state)
        return 0

    lax.fori_loop(0, qi // 2, pair, 0)

    @pl.when(qi % 2 == 1)
    def _():
        s_ref[1] = scores(qi)
        _softmax_update(s_ref[0], values(qi - 1), *state)

    row = lax.broadcasted_iota(jnp.int32, (ATT_TQ, ATT_TK), 0)
    col = lax.broadcasted_iota(jnp.int32, (ATT_TQ, ATT_TK), 1)
    _softmax_update(jnp.where(row >= col, s_ref[qi % 2], MASK), values(qi), *state)
    o_ref[...] = acc_ref[...] / l_ref[...]


def mla_attention(q_all, kv, k_r, batch, seq):
    nq = seq // ATT_TQ
    return pl.pallas_call(
        _mla_kernel,
        grid=(batch, N_HEADS, nq),
        in_specs=[pl.BlockSpec((ATT_TQ, HEAD_DIM), lambda b, h, i: (b * nq + i, h)),
                  pl.BlockSpec((ATT_TQ, HEAD_DIM), lambda b, h, i: (b * nq + i, N_HEADS + h)),
                  pl.BlockSpec((seq, HEAD_DIM), lambda b, h, i: (b, 2 * h)),
                  pl.BlockSpec((seq, HEAD_DIM), lambda b, h, i: (b, 0)),
                  pl.BlockSpec((seq, HEAD_DIM), lambda b, h, i: (b, 2 * h + 1))],
        out_specs=pl.BlockSpec((ATT_TQ, HEAD_DIM), lambda b, h, i: (b * nq + i, h)),
        out_shape=jax.ShapeDtypeStruct((batch * seq, WIDTH), F32),
        scratch_shapes=[pltpu.VMEM((seq, 2 * HEAD_DIM), BF16)] + _softmax_scratch(),
        compiler_params=_params(3),
        name="mla_attention",
    )(q_all, q_all, kv, k_r, kv)


def _dilated_bias(off, neg_slope):
    row = lax.broadcasted_iota(jnp.int32, (ATT_TQ, ATT_TK), 0)
    col = lax.broadcasted_iota(jnp.int32, (ATT_TQ, ATT_TK), 1)
    dist = row - col + off * ATT_TK
    mult = jnp.zeros((ATT_TQ, ATT_TK), jnp.int32)
    for window, dilation in DIL_CONFIGS:
        assert dilation & (dilation - 1) == 0
        held = (dist >= 0) & (dist <= window) & ((dist & (dilation - 1)) == 0)
        mult = mult + held.astype(jnp.int32)
    log2_mult = jnp.full((ATT_TQ, ATT_TK), MASK, F32)
    for n in range(1, len(DIL_CONFIGS) + 1):
        log2_mult = jnp.where(mult == n, jnp.float32(math.log2(n)), log2_mult)
    return neg_slope * dist.astype(F32) + log2_mult


def _dilated_kernel(slope_ref, q_ref, k_ref, v_ref, o_ref, bias_ref, s_ref, m_ref, l_ref, acc_ref):
    h = pl.program_id(1)
    qi = pl.program_id(2)

    @pl.when(qi == 0)
    def _():
        neg_slope = -slope_ref[h] * LOG2E
        for off in range(DIL_NOFF):
            bias_ref[off] = _dilated_bias(off, neg_slope)

    q = q_ref[...]
    state = (m_ref, l_ref, acc_ref)
    _softmax_reset(*state)

    def scores(off):
        kj = pl.multiple_of((qi - off) * ATT_TK, ATT_TK)
        return _qk(q, k_ref[pl.ds(kj, ATT_TK), :]) + bias_ref[off]

    def values(off):
        return v_ref[pl.ds(pl.multiple_of((qi - off) * ATT_TK, ATT_TK), ATT_TK), :]

    n_far = jnp.minimum(qi, DIL_NOFF - 1)
    s_ref[0] = scores(n_far)

    def pair(tt, _):
        off = n_far - 2 * tt
        s_ref[1] = scores(off - 1)
        _softmax_update(s_ref[0], values(off), *state)
        s_ref[0] = scores(off - 2)
        _softmax_update(s_ref[1], values(off - 1), *state)
        return 0

    lax.fori_loop(0, n_far // 2, pair, 0)

    @pl.when(n_far % 2 == 1)
    def _():
        s_ref[1] = scores(jnp.int32(0))
        _softmax_update(s_ref[0], values(jnp.int32(1)), *state)

    _softmax_update(s_ref[n_far % 2], values(jnp.int32(0)), *state)
    o_ref[...] = acc_ref[...] / l_ref[...]


def dilated_attention(qkv, slopes, batch, seq):
    nq = seq // ATT_TQ
    return pl.pallas_call(
        _dilated_kernel,
        grid=(batch, N_HEADS, nq),
        in_specs=[pl.BlockSpec(memory_space=pltpu.SMEM),
                  pl.BlockSpec((ATT_TQ, HEAD_DIM), lambda b, h, i: (b * nq + i, h)),
                  pl.BlockSpec((seq, HEAD_DIM), lambda b, h, i: (b, N_HEADS + h)),
                  pl.BlockSpec((seq, HEAD_DIM), lambda b, h, i: (b, 2 * N_HEADS + h))],
        out_specs=pl.BlockSpec((ATT_TQ, HEAD_DIM), lambda b, h, i: (b * nq + i, h)),
        out_shape=jax.ShapeDtypeStruct((batch * seq, WIDTH), F32),
        scratch_shapes=[pltpu.VMEM((DIL_NOFF, ATT_TQ, ATT_TK), F32)] + _softmax_scratch(),
        compiler_params=_params(3),
        name="dilated_attention",
    )(slopes, qkv, qkv, qkv)


def kernel(x, positions, ffn1_norm, ffn1_w_gate, ffn1_w_up, ffn1_w_down, mix_norm, w_in, g_cq, w_uq,
           g_ckv, w_ukv, g_out_a, g_out_b, w_o, ffn2_norm, ffn2_w_gate, ffn2_w_up, ffn2_w_down,
           final_norm):
    batch, seq, d = x.shape
    m = batch * seq
    depth = w_in.shape[0]

    inv_freq = 1.0 / (ROPE_THETA ** (jnp.arange(0, ROPE_DIM, 2, dtype=F32) / ROPE_DIM))
    ang = positions.astype(F32)[..., None] * inv_freq
    cos = jnp.cos(ang).reshape(m, ROPE_DIM // 2)
    sin = jnp.sin(ang).reshape(m, ROPE_DIM // 2)
    zeros = jnp.zeros((m, 128 - ROPE_DIM), F32)
    cos_p = jnp.concatenate([cos, cos, zeros], axis=-1)
    sin_p = jnp.concatenate([-sin, sin, zeros], axis=-1)

    slopes = jnp.asarray(
        np.array([2.0 ** (-8.0 * (i + 1) / N_HEADS) for i in range(N_HEADS)], dtype=np.float32))
    scale_a = HEAD_DIM ** -0.5 * LOG2E
    scale_b = (HEAD_DIM + ROPE_DIM) ** -0.5 * LOG2E

    w_in_t = jnp.swapaxes(w_in, 1, 2)
    w_down1 = ffn1_w_down.astype(BF16)
    w_down2 = ffn2_w_down.astype(BF16)

    h = x.reshape(m, d)
    for l in range(depth):
        n = rmsnorm(h, ffn1_norm[l], BF16)
        hid = ffn_up(n, ffn1_w_gate, ffn1_w_up, l)
        h = matmul_ksplit_res(hid, w_down1, l, h, 0.5)

        n = rmsnorm(h, mix_norm[l], BF16)
        qkv_a = matmul_fullk_t(n, w_in_t, l, out_dtype=BF16, n_cols=3 * WIDTH, scale=scale_a,
                               n_scaled=WIDTH // 512)
        w_c_t = jnp.concatenate(
            [w_in_t[l, 3 * WIDTH:, :], jnp.zeros((128 - ROPE_DIM, d), F32)], axis=0)
        cq_n, ckv_n, k_r = proj_c(n, w_c_t, g_cq[l], g_ckv[l], cos_p, sin_p)

        out_a = dilated_attention(qkv_a, slopes, batch, seq)

        wq = w_uq[l].reshape(Q_LORA, N_HEADS, HEAD_DIM + ROPE_DIM)
        wq_rope = jnp.pad(wq[:, :, HEAD_DIM:], ((0, 0), (0, 0), (0, 128 - ROPE_DIM)))
        w_q = jnp.concatenate([wq[:, :, :HEAD_DIM].reshape(Q_LORA, WIDTH),
                               wq_rope.reshape(Q_LORA, N_HEADS * 128)], axis=-1)
        q_all = q_up(cq_n, w_q, cos_p, sin_p, scale_b)
        kv = matmul_fullk(ckv_n, w_ukv, l, out_dtype=BF16, tn=1024)
        out_b = mla_attention(q_all, kv, k_r, batch, seq)

        mixed = outnorm(out_a, out_b, g_out_a[l], g_out_b[l])
        h = matmul_fullk(mixed, w_o, l, out_dtype=F32, residual=h, scale=1.0)

        n = rmsnorm(h, ffn2_norm[l], BF16)
        hid = ffn_up(n, ffn2_w_gate, ffn2_w_up, l)
        h = matmul_ksplit_res(hid, w_down2, l, h, 0.5)

    return rmsnorm(h, final_norm, F32).reshape(batch, seq, d)
```

```python
import functools
import math

import numpy as np
import jax
import jax.numpy as jnp
from jax import lax
from jax.experimental import pallas as pl
from jax.experimental.pallas import tpu as pltpu

F32 = jnp.float32
BF16 = jnp.bfloat16

HEAD_DIM = 128
N_HEADS = 16
WIDTH = N_HEADS * HEAD_DIM
DIL_CONFIGS = ((128, 1), (512, 4), (2048, 16))
ROPE_DIM = 64
Q_LORA = 1024
KV_LORA = 512
ROPE_THETA = 10000.0
EPS = 1e-6
MASK = -1e30
LOG2E = math.log2(math.e)

VMEM_LIMIT = 56 * 1024 * 1024
ATT_TQ = 512
ATT_TK = 512
DIL_MAX_DIST = max(w for w, _ in DIL_CONFIGS)
DIL_NOFF = DIL_MAX_DIST // ATT_TK + 1


def _params(n_axes):
    return pltpu.CompilerParams(
        dimension_semantics=("arbitrary",) * n_axes, vmem_limit_bytes=VMEM_LIMIT)


def _rmsnorm_kernel(x_ref, g_ref, o_ref):
    x = x_ref[...]
    ms = jnp.mean(x * x, axis=-1, keepdims=True)
    o_ref[...] = ((x * lax.rsqrt(ms + EPS)) * g_ref[...]).astype(o_ref.dtype)


def rmsnorm(x, g, out_dtype, tm=256):
    m, d = x.shape
    return pl.pallas_call(
        _rmsnorm_kernel,
        grid=(m // tm,),
        in_specs=[pl.BlockSpec((tm, d), lambda i: (i, 0)),
                  pl.BlockSpec((1, d), lambda i: (0, 0))],
        out_specs=pl.BlockSpec((tm, d), lambda i: (i, 0)),
        out_shape=jax.ShapeDtypeStruct((m, d), out_dtype),
        compiler_params=_params(1),
        name="rmsnorm",
    )(x, g.reshape(1, d))


def _outnorm_kernel(a_ref, b_ref, ga_ref, gb_ref, o_ref):
    def norm(x, g):
        ms = jnp.mean(x * x, axis=-1, keepdims=True)
        return ((x * lax.rsqrt(ms + EPS)) * g).astype(o_ref.dtype)
    o_ref[:, :WIDTH] = norm(a_ref[...], ga_ref[...])
    o_ref[:, WIDTH:] = norm(b_ref[...], gb_ref[...])


def outnorm(out_a, out_b, g_a, g_b, tm=256):
    m = out_a.shape[0]
    return pl.pallas_call(
        _outnorm_kernel,
        grid=(m // tm,),
        in_specs=[pl.BlockSpec((tm, WIDTH), lambda i: (i, 0)),
                  pl.BlockSpec((tm, WIDTH), lambda i: (i, 0)),
                  pl.BlockSpec((1, WIDTH), lambda i: (0, 0)),
                  pl.BlockSpec((1, WIDTH), lambda i: (0, 0))],
        out_specs=pl.BlockSpec((tm, 2 * WIDTH), lambda i: (i, 0)),
        out_shape=jax.ShapeDtypeStruct((m, 2 * WIDTH), BF16),
        compiler_params=_params(1),
        name="outnorm",
    )(out_a, out_b, g_a.reshape(1, WIDTH), g_b.reshape(1, WIDTH))


def _dot(a, w):
    return jnp.dot(a, w.astype(BF16), preferred_element_type=F32)


def _dot_t(a, w_t):
    return lax.dot_general(a, w_t.astype(BF16), (((1,), (1,)), ((), ())), preferred_element_type=F32)


def _w_spec(w, layer, block, index_map):
    if w.ndim == 3:
        return pl.BlockSpec((None,) + block, lambda *g: (layer,) + index_map(*g))
    return pl.BlockSpec(block, index_map)


def _resident_spec(tm, kdim):
    return pl.BlockSpec((tm, kdim), lambda i, j: (i, 0), pipeline_mode=pl.Buffered(1))


def _mm_fullk_kernel(a_ref, w_ref, o_ref, *, scale, n_scaled):
    s = jnp.where(pl.program_id(1) < n_scaled, jnp.float32(scale), jnp.float32(1.0))
    o_ref[...] = (_dot(a_ref[...], w_ref[...]) * s).astype(o_ref.dtype)


def _mm_fullk_res_kernel(a_ref, w_ref, r_ref, o_ref, *, scale):
    o_ref[...] = r_ref[...] + scale * _dot(a_ref[...], w_ref[...])


def _mm_fullk_t_kernel(a_ref, w_ref, o_ref, *, scale, n_scaled):
    s = jnp.where(pl.program_id(1) < n_scaled, jnp.float32(scale), jnp.float32(1.0))
    o_ref[...] = (_dot_t(a_ref[...], w_ref[...]) * s).astype(o_ref.dtype)


def matmul_fullk_t(a, w_t, layer, *, out_dtype, n_cols, scale=1.0, n_scaled=0, tm=2048, tn=512):
    m, kdim = a.shape
    tm = min(tm, m)
    return pl.pallas_call(
        functools.partial(_mm_fullk_t_kernel, scale=scale, n_scaled=n_scaled),
        grid=(m // tm, n_cols // tn),
        in_specs=[_resident_spec(tm, kdim),
                  pl.BlockSpec((None, tn, kdim), lambda i, j: (layer, j, 0))],
        out_specs=pl.BlockSpec((tm, tn), lambda i, j: (i, j)),
        out_shape=jax.ShapeDtypeStruct((m, n_cols), out_dtype),
        compiler_params=_params(2),
        name="matmul_fullk_t",
    )(a, w_t)


def matmul_fullk(a, w, layer, *, out_dtype, n_cols=None, scale=1.0, n_scaled=0, residual=None,
                 tm=2048, tn=256):
    m, kdim = a.shape
    n = w.shape[-1] if n_cols is None else n_cols
    tn = min(tn, n)
    tm = min(tm, m)
    a_spec = _resident_spec(tm, kdim)
    w_spec = _w_spec(w, layer, (kdim, tn), lambda i, j: (0, j))
    o_spec = pl.BlockSpec((tm, tn), lambda i, j: (i, j))
    if residual is None:
        body = functools.partial(_mm_fullk_kernel, scale=scale, n_scaled=n_scaled)
        in_specs, args = [a_spec, w_spec], (a, w)
    else:
        body = functools.partial(_mm_fullk_res_kernel, scale=scale)
        in_specs, args = [a_spec, w_spec, o_spec], (a, w, residual)
    return pl.pallas_call(
        body,
        grid=(m // tm, n // tn),
        in_specs=in_specs,
        out_specs=o_spec,
        out_shape=jax.ShapeDtypeStruct((m, n), out_dtype),
        compiler_params=_params(2),
        name="matmul_fullk_res" if residual is not None else "matmul_fullk",
    )(*args)


def _ffn_up_kernel(a_ref, wg_ref, wu_ref, wd_ref, o_ref, wd_bf16_ref, *, n_chunks):
    step = pl.program_id(0) * pl.num_programs(1) + pl.program_id(1)

    @pl.when(step < n_chunks)
    def _():
        wd_bf16_ref[...] = wd_ref[...].astype(BF16)

    a = a_ref[...]
    g = _dot(a, wg_ref[...])
    u = _dot(a, wu_ref[...])
    o_ref[...] = (g * jax.nn.sigmoid(g) * u).astype(o_ref.dtype)


def ffn_up(a, w_gate, w_up, w_down, layer, tm=2048, tn=256):
    m, kdim = a.shape
    n = w_gate.shape[-1]
    tm = min(tm, m)
    n_j = n // tn
    n_steps = (m // tm) * n_j
    chunk = next(c for c in range(128, n + 1, 128) if n % c == 0 and n // c <= n_steps)
    n_chunks = n // chunk
    d_out = w_down.shape[-1]
    chunk_of = lambda i, j: jnp.minimum(i * n_j + j, n_chunks - 1)
    w_spec = _w_spec(w_gate, layer, (kdim, tn), lambda i, j: (0, j))
    return pl.pallas_call(
        functools.partial(_ffn_up_kernel, n_chunks=n_chunks),
        grid=(m // tm, n_j),
        in_specs=[_resident_spec(tm, kdim), w_spec, w_spec,
                  pl.BlockSpec((None, chunk, d_out), lambda i, j: (layer, chunk_of(i, j), 0))],
        out_specs=[pl.BlockSpec((tm, tn), lambda i, j: (i, j)),
                   pl.BlockSpec((chunk, d_out), lambda i, j: (chunk_of(i, j), 0))],
        out_shape=[jax.ShapeDtypeStruct((m, n), BF16),
                   jax.ShapeDtypeStruct((n, d_out), BF16)],
        compiler_params=_params(2),
        name="ffn_up",
    )(a, w_gate, w_up, w_down)


def _mm_ksplit_res_kernel(a_ref, w_ref, r_ref, o_ref, acc_ref, *, nk, k_rem, scale):
    k = pl.program_id(2)

    @pl.when(k == 0)
    def _():
        acc_ref[...] = _dot(a_ref[...], w_ref[...])

    @pl.when(jnp.logical_and(k > 0, k < nk - 1))
    def _():
        acc_ref[...] += _dot(a_ref[...], w_ref[...])

    @pl.when(k == nk - 1)
    def _():
        tail = _dot(a_ref[:, :k_rem], w_ref[:k_rem, :])
        o_ref[...] = r_ref[...] + scale * (acc_ref[...] + tail)


def matmul_ksplit_res(a, w, layer, residual, scale, tm=1024, tn=1024, tk=2816):
    m, kdim = a.shape
    n = w.shape[-1]
    tn = min(tn, n)
    tk = min(tk, kdim)
    nk = pl.cdiv(kdim, tk)
    assert nk >= 2
    k_rem = kdim - (nk - 1) * tk
    o_spec = pl.BlockSpec((tm, tn), lambda j, i, k: (i, j))
    return pl.pallas_call(
        functools.partial(_mm_ksplit_res_kernel, nk=nk, k_rem=k_rem, scale=scale),
        grid=(n // tn, m // tm, nk),
        in_specs=[pl.BlockSpec((tm, tk), lambda j, i, k: (i, k)),
                  _w_spec(w, layer, (tk, tn), lambda j, i, k: (k, j)),
                  o_spec],
        out_specs=o_spec,
        out_shape=jax.ShapeDtypeStruct((m, n), F32),
        scratch_shapes=[pltpu.VMEM((tm, tn), F32)],
        compiler_params=_params(3),
        name="matmul_ksplit_res",
    )(a, w, residual)


def _rope_slab(x, cos_p, sin_p):
    lane = lax.broadcasted_iota(jnp.int32, x.shape, 1)
    half = ROPE_DIM // 2
    swapped = jnp.where(lane < half, pltpu.roll(x, 128 - half, 1), pltpu.roll(x, half, 1))
    return x * cos_p + swapped * sin_p


def _proj_c_kernel(a_ref, w_ref, gq_ref, gkv_ref, cos_ref, sin_ref,
                   cq_ref, ckv_ref, kr_ref, acc_ref, *, nk):
    k = pl.program_id(1)

    @pl.when(k == 0)
    def _():
        acc_ref[...] = _dot_t(a_ref[...], w_ref[...])

    @pl.when(k > 0)
    def _():
        acc_ref[...] += _dot_t(a_ref[...], w_ref[...])

    @pl.when(k == nk - 1)
    def _():
        def norm(x, g):
            ms = jnp.mean(x * x, axis=-1, keepdims=True)
            return (x * lax.rsqrt(ms + EPS)) * g
        cq_ref[...] = norm(acc_ref[:, :Q_LORA], gq_ref[...]).astype(cq_ref.dtype)
        ckv_ref[...] = norm(acc_ref[:, Q_LORA:Q_LORA + KV_LORA], gkv_ref[...]).astype(ckv_ref.dtype)
        kr = acc_ref[:, Q_LORA + KV_LORA:]
        kr_ref[...] = _rope_slab(kr, cos_ref[...], sin_ref[...]).astype(kr_ref.dtype)


def proj_c(a, w_c_t, g_cq, g_ckv, cos_p, sin_p, tm=512, tk=1024):
    m, kdim = a.shape
    n = w_c_t.shape[0]
    tk = min(tk, kdim)
    nk = kdim // tk
    row = lambda width: pl.BlockSpec((tm, width), lambda i, k: (i, 0))
    vec = lambda width: pl.BlockSpec((1, width), lambda i, k: (0, 0))
    return pl.pallas_call(
        functools.partial(_proj_c_kernel, nk=nk),
        grid=(m // tm, nk),
        in_specs=[pl.BlockSpec((tm, tk), lambda i, k: (i, k)),
                  pl.BlockSpec((n, tk), lambda i, k: (0, k)),
                  vec(Q_LORA), vec(KV_LORA), row(128), row(128)],
        out_specs=[row(Q_LORA), row(KV_LORA), row(128)],
        out_shape=[jax.ShapeDtypeStruct((m, Q_LORA), BF16),
                   jax.ShapeDtypeStruct((m, KV_LORA), BF16),
                   jax.ShapeDtypeStruct((m, 128), BF16)],
        scratch_shapes=[pltpu.VMEM((tm, n), F32)],
        compiler_params=_params(2),
        name="proj_c",
    )(a, w_c_t, g_cq.reshape(1, Q_LORA), g_ckv.reshape(1, KV_LORA), cos_p, sin_p)


def _q_up_kernel(a_ref, w_ref, cos_ref, sin_ref, o_ref, *, scale, n_nope_tiles):
    x = _dot(a_ref[...], w_ref[...])
    j = pl.program_id(1)

    @pl.when(j < n_nope_tiles)
    def _():
        o_ref[...] = (x * scale).astype(o_ref.dtype)

    @pl.when(j >= n_nope_tiles)
    def _():
        cos_p = cos_ref[...]
        sin_p = sin_ref[...]
        for s in range(x.shape[1] // 128):
            slab = x[:, s * 128:(s + 1) * 128]
            o_ref[:, s * 128:(s + 1) * 128] = (_rope_slab(slab, cos_p, sin_p) * scale).astype(o_ref.dtype)


def q_up(a, w_q, cos_p, sin_p, scale, tm=1024, tn=1024):
    m, kdim = a.shape
    n = w_q.shape[1]
    return pl.pallas_call(
        functools.partial(_q_up_kernel, scale=scale, n_nope_tiles=WIDTH // tn),
        grid=(m // tm, n // tn),
        in_specs=[pl.BlockSpec((tm, kdim), lambda i, j: (i, 0)),
                  pl.BlockSpec((kdim, tn), lambda i, j: (0, j)),
                  pl.BlockSpec((tm, 128), lambda i, j: (i, 0)),
                  pl.BlockSpec((tm, 128), lambda i, j: (i, 0))],
        out_specs=pl.BlockSpec((tm, tn), lambda i, j: (i, j)),
        out_shape=jax.ShapeDtypeStruct((m, n), BF16),
        compiler_params=_params(2),
        name="q_up",
    )(a, w_q, cos_p, sin_p)


def _qk(q, k):
    return lax.dot_general(q, k, (((1,), (1,)), ((), ())), preferred_element_type=F32)


def _softmax_update(s, v, m_ref, l_ref, acc_ref):
    m_prev = m_ref[...]
    m_new = jnp.maximum(m_prev, jnp.max(s, axis=-1, keepdims=True))
    alpha = jnp.exp2(m_prev - m_new)
    p = jnp.exp2(s - jnp.concatenate([m_new] * (s.shape[1] // 128), axis=1))
    l_ref[...] = alpha * l_ref[...] + jnp.sum(p, axis=-1, keepdims=True)
    acc_ref[...] = alpha * acc_ref[...] + jnp.dot(p.astype(BF16), v, preferred_element_type=F32)
    m_ref[...] = m_new


def _softmax_reset(m_ref, l_ref, acc_ref):
    m_ref[...] = jnp.full(m_ref.shape, MASK, F32)
    l_ref[...] = jnp.zeros(l_ref.shape, F32)
    acc_ref[...] = jnp.zeros(acc_ref.shape, F32)


def _softmax_scratch():
    return [pltpu.VMEM((2, ATT_TQ, ATT_TK), F32), pltpu.VMEM((ATT_TQ, 128), F32),
            pltpu.VMEM((ATT_TQ, 128), F32), pltpu.VMEM((ATT_TQ, HEAD_DIM), F32)]


def _mla_kernel(qn_ref, qr_ref, kn_ref, kr_ref, v_ref, o_ref, kc_ref, s_ref, m_ref, l_ref, acc_ref):
    qi = pl.program_id(2)

    @pl.when(qi == 0)
    def _():
        kc_ref[:, :HEAD_DIM] = kn_ref[...]
        kc_ref[:, HEAD_DIM:] = kr_ref[...]

    q = jnp.concatenate([qn_ref[...], qr_ref[...]], axis=-1)
    state = (m_ref, l_ref, acc_ref)
    _softmax_reset(*state)

    def scores(j):
        return _qk(q, kc_ref[pl.ds(pl.multiple_of(j * ATT_TK, ATT_TK), ATT_TK), :])

    def values(j):
        return v_ref[pl.ds(pl.multiple_of(j * ATT_TK, ATT_TK), ATT_TK), :]

    s_ref[0] = scores(jnp.int32(0))

    def pair(jj, _):
        j = 2 * jj
        s_ref[1] = scores(j + 1)
        _softmax_update(s_ref[0], values(j), *state)
        s_ref[0] = scores(j + 2)
        _softmax_update(s_ref[1], values(j + 1), *state)
        return 0

    lax.fori_loop(0, qi // 2, pair, 0)

    @pl.when(qi % 2 == 1)
    def _():
        s_ref[1] = scores(qi)
        _softmax_update(s_ref[0], values(qi - 1), *state)

    row = lax.broadcasted_iota(jnp.int32, (ATT_TQ, ATT_TK), 0)
    col = lax.broadcasted_iota(jnp.int32, (ATT_TQ, ATT_TK), 1)
    _softmax_update(jnp.where(row >= col, s_ref[qi % 2], MASK), values(qi), *state)
    o_ref[...] = acc_ref[...] / l_ref[...]


def mla_attention(q_all, kv, k_r, batch, seq):
    nq = seq // ATT_TQ
    return pl.pallas_call(
        _mla_kernel,
        grid=(batch, N_HEADS, nq),
        in_specs=[pl.BlockSpec((ATT_TQ, HEAD_DIM), lambda b, h, i: (b * nq + i, h)),
                  pl.BlockSpec((ATT_TQ, HEAD_DIM), lambda b, h, i: (b * nq + i, N_HEADS + h)),
                  pl.BlockSpec((seq, HEAD_DIM), lambda b, h, i: (b, 2 * h)),
                  pl.BlockSpec((seq, HEAD_DIM), lambda b, h, i: (b, 0)),
                  pl.BlockSpec((seq, HEAD_DIM), lambda b, h, i: (b, 2 * h + 1))],
        out_specs=pl.BlockSpec((ATT_TQ, HEAD_DIM), lambda b, h, i: (b * nq + i, h)),
        out_shape=jax.ShapeDtypeStruct((batch * seq, WIDTH), F32),
        scratch_shapes=[pltpu.VMEM((seq, 2 * HEAD_DIM), BF16)] + _softmax_scratch(),
        compiler_params=_params(3),
        name="mla_attention",
    )(q_all, q_all, kv, k_r, kv)


def _dilated_bias(off, neg_slope):
    row = lax.broadcasted_iota(jnp.int32, (ATT_TQ, ATT_TK), 0)
    col = lax.broadcasted_iota(jnp.int32, (ATT_TQ, ATT_TK), 1)
    dist = row - col + off * ATT_TK
    mult = jnp.zeros((ATT_TQ, ATT_TK), jnp.int32)
    for window, dilation in DIL_CONFIGS:
        assert dilation & (dilation - 1) == 0
        held = (dist >= 0) & (dist <= window) & ((dist & (dilation - 1)) == 0)
        mult = mult + held.astype(jnp.int32)
    log2_mult = jnp.full((ATT_TQ, ATT_TK), MASK, F32)
    for n in range(1, len(DIL_CONFIGS) + 1):
        log2_mult = jnp.where(mult == n, jnp.float32(math.log2(n)), log2_mult)
    return neg_slope * dist.astype(F32) + log2_mult


def _dilated_kernel(slope_ref, q_ref, k_ref, v_ref, o_ref, bias_ref, s_ref, m_ref, l_ref, acc_ref):
    h = pl.program_id(1)
    qi = pl.program_id(2)

    @pl.when(qi == 0)
    def _():
        neg_slope = -slope_ref[h] * LOG2E
        for off in range(DIL_NOFF):
            bias_ref[off] = _dilated_bias(off, neg_slope)

    q = q_ref[...]
    state = (m_ref, l_ref, acc_ref)
    _softmax_reset(*state)

    def scores(off):
        kj = pl.multiple_of((qi - off) * ATT_TK, ATT_TK)
        return _qk(q, k_ref[pl.ds(kj, ATT_TK), :]) + bias_ref[off]

    def values(off):
        return v_ref[pl.ds(pl.multiple_of((qi - off) * ATT_TK, ATT_TK), ATT_TK), :]

    n_far = jnp.minimum(qi, DIL_NOFF - 1)
    s_ref[0] = scores(n_far)

    def pair(tt, _):
        off = n_far - 2 * tt
        s_ref[1] = scores(off - 1)
        _softmax_update(s_ref[0], values(off), *state)
        s_ref[0] = scores(off - 2)
        _softmax_update(s_ref[1], values(off - 1), *state)
        return 0

    lax.fori_loop(0, n_far // 2, pair, 0)

    @pl.when(n_far % 2 == 1)
    def _():
        s_ref[1] = scores(jnp.int32(0))
        _softmax_update(s_ref[0], values(jnp.int32(1)), *state)

    _softmax_update(s_ref[n_far % 2], values(jnp.int32(0)), *state)
    o_ref[...] = acc_ref[...] / l_ref[...]


def dilated_attention(qkv, slopes, batch, seq):
    nq = seq // ATT_TQ
    return pl.pallas_call(
        _dilated_kernel,
        grid=(batch, N_HEADS, nq),
        in_specs=[pl.BlockSpec(memory_space=pltpu.SMEM),
                  pl.BlockSpec((ATT_TQ, HEAD_DIM), lambda b, h, i: (b * nq + i, h)),
                  pl.BlockSpec((seq, HEAD_DIM), lambda b, h, i: (b, N_HEADS + h)),
                  pl.BlockSpec((seq, HEAD_DIM), lambda b, h, i: (b, 2 * N_HEADS + h))],
        out_specs=pl.BlockSpec((ATT_TQ, HEAD_DIM), lambda b, h, i: (b * nq + i, h)),
        out_shape=jax.ShapeDtypeStruct((batch * seq, WIDTH), F32),
        scratch_shapes=[pltpu.VMEM((DIL_NOFF, ATT_TQ, ATT_TK), F32)] + _softmax_scratch(),
        compiler_params=_params(3),
        name="dilated_attention",
    )(slopes, qkv, qkv, qkv)


def kernel(x, positions, ffn1_norm, ffn1_w_gate, ffn1_w_up, ffn1_w_down, mix_norm, w_in, g_cq, w_uq,
           g_ckv, w_ukv, g_out_a, g_out_b, w_o, ffn2_norm, ffn2_w_gate, ffn2_w_up, ffn2_w_down,
           final_norm):
    batch, seq, d = x.shape
    m = batch * seq
    depth = w_in.shape[0]

    inv_freq = 1.0 / (ROPE_THETA ** (jnp.arange(0, ROPE_DIM, 2, dtype=F32) / ROPE_DIM))
    ang = positions.astype(F32)[..., None] * inv_freq
    cos = jnp.cos(ang).reshape(m, ROPE_DIM // 2)
    sin = jnp.sin(ang).reshape(m, ROPE_DIM // 2)
    zeros = jnp.zeros((m, 128 - ROPE_DIM), F32)
    cos_p = jnp.concatenate([cos, cos, zeros], axis=-1)
    sin_p = jnp.concatenate([-sin, sin, zeros], axis=-1)

    slopes = jnp.asarray(
        np.array([2.0 ** (-8.0 * (i + 1) / N_HEADS) for i in range(N_HEADS)], dtype=np.float32))
    scale_a = HEAD_DIM ** -0.5 * LOG2E
    scale_b = (HEAD_DIM + ROPE_DIM) ** -0.5 * LOG2E

    w_in_t = jnp.swapaxes(w_in, 1, 2)

    h = x.reshape(m, d)
    for l in range(depth):
        n = rmsnorm(h, ffn1_norm[l], BF16)
        hid, w_down = ffn_up(n, ffn1_w_gate, ffn1_w_up, ffn1_w_down, l)
        h = matmul_ksplit_res(hid, w_down, None, h, 0.5)

        n = rmsnorm(h, mix_norm[l], BF16)
        qkv_a = matmul_fullk_t(n, w_in_t, l, out_dtype=BF16, n_cols=3 * WIDTH, scale=scale_a,
                               n_scaled=WIDTH // 512)
        w_c_t = jnp.concatenate(
            [w_in_t[l, 3 * WIDTH:, :], jnp.zeros((128 - ROPE_DIM, d), F32)], axis=0)
        cq_n, ckv_n, k_r = proj_c(n, w_c_t, g_cq[l], g_ckv[l], cos_p, sin_p)

        out_a = dilated_attention(qkv_a, slopes, batch, seq)

        wq = w_uq[l].reshape(Q_LORA, N_HEADS, HEAD_DIM + ROPE_DIM)
        wq_rope = jnp.pad(wq[:, :, HEAD_DIM:], ((0, 0), (0, 0), (0, 128 - ROPE_DIM)))
        w_q = jnp.concatenate([wq[:, :, :HEAD_DIM].reshape(Q_LORA, WIDTH),
                               wq_rope.reshape(Q_LORA, N_HEADS * 128)], axis=-1)
        q_all = q_up(cq_n, w_q, cos_p, sin_p, scale_b)
        kv = matmul_fullk(ckv_n, w_ukv, l, out_dtype=BF16, tn=1024)
        out_b = mla_attention(q_all, kv, k_r, batch, seq)

        mixed = outnorm(out_a, out_b, g_out_a[l], g_out_b[l])
        h = matmul_fullk(mixed, w_o, l, out_dtype=F32, residual=h, scale=1.0)

        n = rmsnorm(h, ffn2_norm[l], BF16)
        hid, w_down = ffn_up(n, ffn2_w_gate, ffn2_w_up, ffn2_w_down, l)
        h = matmul_ksplit_res(hid, w_down, None, h, 0.5)

    return rmsnorm(h, final_norm, F32).reshape(batch, seq, d)
```

```python
import functools
import math

import numpy as np
import jax
import jax.numpy as jnp
from jax import lax
from jax.experimental import pallas as pl
from jax.experimental.pallas import tpu as pltpu

F32 = jnp.float32
BF16 = jnp.bfloat16

HEAD_DIM = 128
N_HEADS = 16
WIDTH = N_HEADS * HEAD_DIM
DIL_CONFIGS = ((128, 1), (512, 4), (2048, 16))
ROPE_DIM = 64
Q_LORA = 1024
KV_LORA = 512
ROPE_THETA = 10000.0
EPS = 1e-6
MASK = -1e30
LOG2E = math.log2(math.e)

VMEM_LIMIT = 56 * 1024 * 1024
ATT_TQ = 512
ATT_TK = 512
DIL_MAX_DIST = max(w for w, _ in DIL_CONFIGS)
DIL_NOFF = DIL_MAX_DIST // ATT_TK + 1


def _params(n_axes):
    return pltpu.CompilerParams(
        dimension_semantics=("arbitrary",) * n_axes, vmem_limit_bytes=VMEM_LIMIT)


def _rmsnorm_kernel(x_ref, g_ref, o_ref):
    x = x_ref[...]
    ms = jnp.mean(x * x, axis=-1, keepdims=True)
    o_ref[...] = ((x * lax.rsqrt(ms + EPS)) * g_ref[...]).astype(o_ref.dtype)


def rmsnorm(x, g, out_dtype, tm=256):
    m, d = x.shape
    return pl.pallas_call(
        _rmsnorm_kernel,
        grid=(m // tm,),
        in_specs=[pl.BlockSpec((tm, d), lambda i: (i, 0)),
                  pl.BlockSpec((1, d), lambda i: (0, 0))],
        out_specs=pl.BlockSpec((tm, d), lambda i: (i, 0)),
        out_shape=jax.ShapeDtypeStruct((m, d), out_dtype),
        compiler_params=_params(1),
        name="rmsnorm",
    )(x, g.reshape(1, d))


def _outnorm_kernel(a_ref, b_ref, ga_ref, gb_ref, o_ref):
    def norm(x, g):
        ms = jnp.mean(x * x, axis=-1, keepdims=True)
        return ((x * lax.rsqrt(ms + EPS)) * g).astype(o_ref.dtype)
    o_ref[:, :WIDTH] = norm(a_ref[...], ga_ref[...])
    o_ref[:, WIDTH:] = norm(b_ref[...], gb_ref[...])


def outnorm(out_a, out_b, g_a, g_b, tm=256):
    m = out_a.shape[0]
    return pl.pallas_call(
        _outnorm_kernel,
        grid=(m // tm,),
        in_specs=[pl.BlockSpec((tm, WIDTH), lambda i: (i, 0)),
                  pl.BlockSpec((tm, WIDTH), lambda i: (i, 0)),
                  pl.BlockSpec((1, WIDTH), lambda i: (0, 0)),
                  pl.BlockSpec((1, WIDTH), lambda i: (0, 0))],
        out_specs=pl.BlockSpec((tm, 2 * WIDTH), lambda i: (i, 0)),
        out_shape=jax.ShapeDtypeStruct((m, 2 * WIDTH), BF16),
        compiler_params=_params(1),
        name="outnorm",
    )(out_a, out_b, g_a.reshape(1, WIDTH), g_b.reshape(1, WIDTH))


def _dot(a, w):
    return jnp.dot(a, w.astype(BF16), preferred_element_type=F32)


def _dot_t(a, w_t):
    return lax.dot_general(a, w_t.astype(BF16), (((1,), (1,)), ((), ())), preferred_element_type=F32)


def _w_spec(w, layer, block, index_map):
    if w.ndim == 3:
        return pl.BlockSpec((None,) + block, lambda *g: (layer,) + index_map(*g))
    return pl.BlockSpec(block, index_map)


def _resident_spec(tm, kdim, single_buffer=True):
    if single_buffer:
        return pl.BlockSpec((tm, kdim), lambda i, j: (i, 0), pipeline_mode=pl.Buffered(1))
    return pl.BlockSpec((tm, kdim), lambda i, j: (i, 0))


def _mm_fullk_kernel(a_ref, w_ref, o_ref, *, scale, n_scaled):
    s = jnp.where(pl.program_id(1) < n_scaled, jnp.float32(scale), jnp.float32(1.0))
    o_ref[...] = (_dot(a_ref[...], w_ref[...]) * s).astype(o_ref.dtype)


def _mm_fullk_res_kernel(a_ref, w_ref, r_ref, o_ref, *, scale):
    o_ref[...] = r_ref[...] + scale * _dot(a_ref[...], w_ref[...])


def _mm_fullk_t_kernel(a_ref, w_ref, o_ref, *, scale, n_scaled):
    s = jnp.where(pl.program_id(1) < n_scaled, jnp.float32(scale), jnp.float32(1.0))
    o_ref[...] = (_dot_t(a_ref[...], w_ref[...]) * s).astype(o_ref.dtype)


def matmul_fullk_t(a, w_t, layer, *, out_dtype, n_cols, scale=1.0, n_scaled=0, tm=2048, tn=512):
    m, kdim = a.shape
    tm = min(tm, m)
    return pl.pallas_call(
        functools.partial(_mm_fullk_t_kernel, scale=scale, n_scaled=n_scaled),
        grid=(m // tm, n_cols // tn),
        in_specs=[_resident_spec(tm, kdim),
                  pl.BlockSpec((None, tn, kdim), lambda i, j: (layer, j, 0))],
        out_specs=pl.BlockSpec((tm, tn), lambda i, j: (i, j)),
        out_shape=jax.ShapeDtypeStruct((m, n_cols), out_dtype),
        compiler_params=_params(2),
        name="matmul_fullk_t",
    )(a, w_t)


def matmul_fullk(a, w, layer, *, out_dtype, n_cols=None, scale=1.0, n_scaled=0, residual=None,
                 tm=2048, tn=256, single_buffer=True):
    m, kdim = a.shape
    n = w.shape[-1] if n_cols is None else n_cols
    tn = min(tn, n)
    tm = min(tm, m)
    a_spec = _resident_spec(tm, kdim, single_buffer)
    w_spec = _w_spec(w, layer, (kdim, tn), lambda i, j: (0, j))
    o_spec = pl.BlockSpec((tm, tn), lambda i, j: (i, j))
    if residual is None:
        body = functools.partial(_mm_fullk_kernel, scale=scale, n_scaled=n_scaled)
        in_specs, args = [a_spec, w_spec], (a, w)
    else:
        body = functools.partial(_mm_fullk_res_kernel, scale=scale)
        in_specs, args = [a_spec, w_spec, o_spec], (a, w, residual)
    return pl.pallas_call(
        body,
        grid=(m // tm, n // tn),
        in_specs=in_specs,
        out_specs=o_spec,
        out_shape=jax.ShapeDtypeStruct((m, n), out_dtype),
        compiler_params=_params(2),
        name="matmul_fullk_res" if residual is not None else "matmul_fullk",
    )(*args)


def _ffn_up_kernel(a_ref, wg_ref, wu_ref, wd_ref, o_ref, wd_bf16_ref, *, n_chunks):
    step = pl.program_id(0) * pl.num_programs(1) + pl.program_id(1)

    @pl.when(step < n_chunks)
    def _():
        wd_bf16_ref[...] = wd_ref[...].astype(BF16)

    a = a_ref[...]
    g = _dot(a, wg_ref[...])
    u = _dot(a, wu_ref[...])
    o_ref[...] = (g * jax.nn.sigmoid(g) * u).astype(o_ref.dtype)


def ffn_up(a, w_gate, w_up, w_down, layer, tm=2048, tn=256):
    m, kdim = a.shape
    n = w_gate.shape[-1]
    tm = min(tm, m)
    n_j = n // tn
    n_steps = (m // tm) * n_j
    chunk = next(c for c in range(128, n + 1, 128) if n % c == 0 and n // c <= n_steps)
    n_chunks = n // chunk
    d_out = w_down.shape[-1]
    chunk_of = lambda i, j: jnp.minimum(i * n_j + j, n_chunks - 1)
    w_spec = _w_spec(w_gate, layer, (kdim, tn), lambda i, j: (0, j))
    return pl.pallas_call(
        functools.partial(_ffn_up_kernel, n_chunks=n_chunks),
        grid=(m // tm, n_j),
        in_specs=[_resident_spec(tm, kdim), w_spec, w_spec,
                  pl.BlockSpec((None, chunk, d_out), lambda i, j: (layer, chunk_of(i, j), 0))],
        out_specs=[pl.BlockSpec((tm, tn), lambda i, j: (i, j)),
                   pl.BlockSpec((chunk, d_out), lambda i, j: (chunk_of(i, j), 0))],
        out_shape=[jax.ShapeDtypeStruct((m, n), BF16),
                   jax.ShapeDtypeStruct((n, d_out), BF16)],
        compiler_params=_params(2),
        name="ffn_up",
    )(a, w_gate, w_up, w_down)


def _mm_ksplit_res_kernel(a_ref, w_ref, r_ref, o_ref, acc_ref, *, nk, k_rem, scale):
    k = pl.program_id(2)

    @pl.when(k == 0)
    def _():
        acc_ref[...] = _dot(a_ref[...], w_ref[...])

    @pl.when(jnp.logical_and(k > 0, k < nk - 1))
    def _():
        acc_ref[...] += _dot(a_ref[...], w_ref[...])

    @pl.when(k == nk - 1)
    def _():
        tail = _dot(a_ref[:, :k_rem], w_ref[:k_rem, :])
        o_ref[...] = r_ref[...] + scale * (acc_ref[...] + tail)


def matmul_ksplit_res(a, w, layer, residual, scale, tm=1024, tn=1024, tk=2816):
    m, kdim = a.shape
    n = w.shape[-1]
    tn = min(tn, n)
    tk = min(tk, kdim)
    nk = pl.cdiv(kdim, tk)
    assert nk >= 2
    k_rem = kdim - (nk - 1) * tk
    o_spec = pl.BlockSpec((tm, tn), lambda j, i, k: (i, j))
    return pl.pallas_call(
        functools.partial(_mm_ksplit_res_kernel, nk=nk, k_rem=k_rem, scale=scale),
        grid=(n // tn, m // tm, nk),
        in_specs=[pl.BlockSpec((tm, tk), lambda j, i, k: (i, k)),
                  _w_spec(w, layer, (tk, tn), lambda j, i, k: (k, j)),
                  o_spec],
        out_specs=o_spec,
        out_shape=jax.ShapeDtypeStruct((m, n), F32),
        scratch_shapes=[pltpu.VMEM((tm, tn), F32)],
        compiler_params=_params(3),
        name="matmul_ksplit_res",
    )(a, w, residual)


def _rope_slab(x, cos_p, sin_p):
    lane = lax.broadcasted_iota(jnp.int32, x.shape, 1)
    half = ROPE_DIM // 2
    swapped = jnp.where(lane < half, pltpu.roll(x, 128 - half, 1), pltpu.roll(x, half, 1))
    return x * cos_p + swapped * sin_p


def _proj_c_kernel(a_ref, w_ref, gq_ref, gkv_ref, cos_ref, sin_ref,
                   cq_ref, ckv_ref, kr_ref, acc_ref, *, nk):
    k = pl.program_id(1)

    @pl.when(k == 0)
    def _():
        acc_ref[...] = _dot_t(a_ref[...], w_ref[...])

    @pl.when(k > 0)
    def _():
        acc_ref[...] += _dot_t(a_ref[...], w_ref[...])

    @pl.when(k == nk - 1)
    def _():
        def norm(x, g):
            ms = jnp.mean(x * x, axis=-1, keepdims=True)
            return (x * lax.rsqrt(ms + EPS)) * g
        cq_ref[...] = norm(acc_ref[:, :Q_LORA], gq_ref[...]).astype(cq_ref.dtype)
        ckv_ref[...] = norm(acc_ref[:, Q_LORA:Q_LORA + KV_LORA], gkv_ref[...]).astype(ckv_ref.dtype)
        kr = acc_ref[:, Q_LORA + KV_LORA:]
        kr_ref[...] = _rope_slab(kr, cos_ref[...], sin_ref[...]).astype(kr_ref.dtype)


def proj_c(a, w_c_t, g_cq, g_ckv, cos_p, sin_p, tm=1024, tk=1024):
    m, kdim = a.shape
    n = w_c_t.shape[0]
    tm = min(tm, m)
    tk = min(tk, kdim)
    nk = kdim // tk
    row = lambda width: pl.BlockSpec((tm, width), lambda i, k: (i, 0))
    vec = lambda width: pl.BlockSpec((1, width), lambda i, k: (0, 0))
    return pl.pallas_call(
        functools.partial(_proj_c_kernel, nk=nk),
        grid=(m // tm, nk),
        in_specs=[pl.BlockSpec((tm, tk), lambda i, k: (i, k)),
                  pl.BlockSpec((n, tk), lambda i, k: (0, k)),
                  vec(Q_LORA), vec(KV_LORA), row(128), row(128)],
        out_specs=[row(Q_LORA), row(KV_LORA), row(128)],
        out_shape=[jax.ShapeDtypeStruct((m, Q_LORA), BF16),
                   jax.ShapeDtypeStruct((m, KV_LORA), BF16),
                   jax.ShapeDtypeStruct((m, 128), BF16)],
        scratch_shapes=[pltpu.VMEM((tm, n), F32)],
        compiler_params=_params(2),
        name="proj_c",
    )(a, w_c_t, g_cq.reshape(1, Q_LORA), g_ckv.reshape(1, KV_LORA), cos_p, sin_p)


def _q_up_kernel(a_ref, w_ref, cos_ref, sin_ref, o_ref, *, scale, n_nope_tiles):
    x = _dot(a_ref[...], w_ref[...])
    j = pl.program_id(1)

    @pl.when(j < n_nope_tiles)
    def _():
        o_ref[...] = (x * scale).astype(o_ref.dtype)

    @pl.when(j >= n_nope_tiles)
    def _():
        cos_p = cos_ref[...]
        sin_p = sin_ref[...]
        for s in range(x.shape[1] // 128):
            slab = x[:, s * 128:(s + 1) * 128]
            o_ref[:, s * 128:(s + 1) * 128] = (_rope_slab(slab, cos_p, sin_p) * scale).astype(o_ref.dtype)


def q_up(a, w_q, cos_p, sin_p, scale, tm=2048, tn=1024):
    m, kdim = a.shape
    tm = min(tm, m)
    n = w_q.shape[1]
    return pl.pallas_call(
        functools.partial(_q_up_kernel, scale=scale, n_nope_tiles=WIDTH // tn),
        grid=(m // tm, n // tn),
        in_specs=[pl.BlockSpec((tm, kdim), lambda i, j: (i, 0)),
                  pl.BlockSpec((kdim, tn), lambda i, j: (0, j)),
                  pl.BlockSpec((tm, 128), lambda i, j: (i, 0)),
                  pl.BlockSpec((tm, 128), lambda i, j: (i, 0))],
        out_specs=pl.BlockSpec((tm, tn), lambda i, j: (i, j)),
        out_shape=jax.ShapeDtypeStruct((m, n), BF16),
        compiler_params=_params(2),
        name="q_up",
    )(a, w_q, cos_p, sin_p)


def _qk(q, k):
    return lax.dot_general(q, k, (((1,), (1,)), ((), ())), preferred_element_type=F32)


def _softmax_update(s, v, m_ref, l_ref, acc_ref):
    m_prev = m_ref[...]
    m_new = jnp.maximum(m_prev, jnp.max(s, axis=-1, keepdims=True))
    alpha = jnp.exp2(m_prev - m_new)
    p = jnp.exp2(s - jnp.concatenate([m_new] * (s.shape[1] // 128), axis=1))
    l_ref[...] = alpha * l_ref[...] + jnp.sum(p, axis=-1, keepdims=True)
    acc_ref[...] = alpha * acc_ref[...] + jnp.dot(p.astype(BF16), v, preferred_element_type=F32)
    m_ref[...] = m_new


def _softmax_reset(m_ref, l_ref, acc_ref):
    m_ref[...] = jnp.full(m_ref.shape, MASK, F32)
    l_ref[...] = jnp.zeros(l_ref.shape, F32)
    acc_ref[...] = jnp.zeros(acc_ref.shape, F32)


def _softmax_scratch():
    return [pltpu.VMEM((2, ATT_TQ, ATT_TK), F32), pltpu.VMEM((ATT_TQ, 128), F32),
            pltpu.VMEM((ATT_TQ, 128), F32), pltpu.VMEM((ATT_TQ, HEAD_DIM), F32)]


def _mla_kernel(qn_ref, qr_ref, kn_ref, kr_ref, v_ref, o_ref, kc_ref, s_ref, m_ref, l_ref, acc_ref):
    qi = pl.program_id(2)

    @pl.when(qi == 0)
    def _():
        kc_ref[:, :HEAD_DIM] = kn_ref[...]
        kc_ref[:, HEAD_DIM:] = kr_ref[...]

    q = jnp.concatenate([qn_ref[...], qr_ref[...]], axis=-1)
    state = (m_ref, l_ref, acc_ref)
    _softmax_reset(*state)

    def scores(j):
        return _qk(q, kc_ref[pl.ds(pl.multiple_of(j * ATT_TK, ATT_TK), ATT_TK), :])

    def values(j):
        return v_ref[pl.ds(pl.multiple_of(j * ATT_TK, ATT_TK), ATT_TK), :]

    s_ref[0] = scores(jnp.int32(0))

    def pair(jj, _):
        j = 2 * jj
        s_ref[1] = scores(j + 1)
        _softmax_update(s_ref[0], values(j), *state)
        s_ref[0] = scores(j + 2)
        _softmax_update(s_ref[1], values(j + 1), *state)
        return 0

    lax.fori_loop(0, qi // 2, pair, 0)

    @pl.when(qi % 2 == 1)
    def _():
        s_ref[1] = scores(qi)
        _softmax_update(s_ref[0], values(qi - 1), *state)

    row = lax.broadcasted_iota(jnp.int32, (ATT_TQ, ATT_TK), 0)
    col = lax.broadcasted_iota(jnp.int32, (ATT_TQ, ATT_TK), 1)
    _softmax_update(jnp.where(row >= col, s_ref[qi % 2], MASK), values(qi), *state)
    o_ref[...] = acc_ref[...] / l_ref[...]


def mla_attention(q_all, kv, k_r, batch, seq):
    nq = seq // ATT_TQ
    return pl.pallas_call(
        _mla_kernel,
        grid=(batch, N_HEADS, nq),
        in_specs=[pl.BlockSpec((ATT_TQ, HEAD_DIM), lambda b, h, i: (b * nq + i, h)),
                  pl.BlockSpec((ATT_TQ, HEAD_DIM), lambda b, h, i: (b * nq + i, N_HEADS + h)),
                  pl.BlockSpec((seq, HEAD_DIM), lambda b, h, i: (b, 2 * h)),
                  pl.BlockSpec((seq, HEAD_DIM), lambda b, h, i: (b, 0)),
                  pl.BlockSpec((seq, HEAD_DIM), lambda b, h, i: (b, 2 * h + 1))],
        out_specs=pl.BlockSpec((ATT_TQ, HEAD_DIM), lambda b, h, i: (b * nq + i, h)),
        out_shape=jax.ShapeDtypeStruct((batch * seq, WIDTH), F32),
        scratch_shapes=[pltpu.VMEM((seq, 2 * HEAD_DIM), BF16)] + _softmax_scratch(),
        compiler_params=_params(3),
        name="mla_attention",
    )(q_all, q_all, kv, k_r, kv)


def _dilated_bias(off, neg_slope):
    row = lax.broadcasted_iota(jnp.int32, (ATT_TQ, ATT_TK), 0)
    col = lax.broadcasted_iota(jnp.int32, (ATT_TQ, ATT_TK), 1)
    dist = row - col + off * ATT_TK
    mult = jnp.zeros((ATT_TQ, ATT_TK), jnp.int32)
    for window, dilation in DIL_CONFIGS:
        assert dilation & (dilation - 1) == 0
        held = (dist >= 0) & (dist <= window) & ((dist & (dilation - 1)) == 0)
        mult = mult + held.astype(jnp.int32)
    log2_mult = jnp.full((ATT_TQ, ATT_TK), MASK, F32)
    for n in range(1, len(DIL_CONFIGS) + 1):
        log2_mult = jnp.where(mult == n, jnp.float32(math.log2(n)), log2_mult)
    return neg_slope * dist.astype(F32) + log2_mult


def _dilated_kernel(slope_ref, q_ref, k_ref, v_ref, o_ref, bias_ref, s_ref, m_ref, l_ref, acc_ref):
    h = pl.program_id(1)
    qi = pl.program_id(2)

    @pl.when(qi == 0)
    def _():
        neg_slope = -slope_ref[h] * LOG2E
        for off in range(DIL_NOFF):
            bias_ref[off] = _dilated_bias(off, neg_slope)

    q = q_ref[...]
    state = (m_ref, l_ref, acc_ref)
    _softmax_reset(*state)

    def scores(off):
        kj = pl.multiple_of((qi - off) * ATT_TK, ATT_TK)
        return _qk(q, k_ref[pl.ds(kj, ATT_TK), :]) + bias_ref[off]

    def values(off):
        return v_ref[pl.ds(pl.multiple_of((qi - off) * ATT_TK, ATT_TK), ATT_TK), :]

    n_far = jnp.minimum(qi, DIL_NOFF - 1)
    s_ref[0] = scores(n_far)

    def pair(tt, _):
        off = n_far - 2 * tt
        s_ref[1] = scores(off - 1)
        _softmax_update(s_ref[0], values(off), *state)
        s_ref[0] = scores(off - 2)
        _softmax_update(s_ref[1], values(off - 1), *state)
        return 0

    lax.fori_loop(0, n_far // 2, pair, 0)

    @pl.when(n_far % 2 == 1)
    def _():
        s_ref[1] = scores(jnp.int32(0))
        _softmax_update(s_ref[0], values(jnp.int32(1)), *state)

    _softmax_update(s_ref[n_far % 2], values(jnp.int32(0)), *state)
    o_ref[...] = acc_ref[...] / l_ref[...]


def dilated_attention(qkv, slopes, batch, seq):
    nq = seq // ATT_TQ
    return pl.pallas_call(
        _dilated_kernel,
        grid=(batch, N_HEADS, nq),
        in_specs=[pl.BlockSpec(memory_space=pltpu.SMEM),
                  pl.BlockSpec((ATT_TQ, HEAD_DIM), lambda b, h, i: (b * nq + i, h)),
                  pl.BlockSpec((seq, HEAD_DIM), lambda b, h, i: (b, N_HEADS + h)),
                  pl.BlockSpec((seq, HEAD_DIM), lambda b, h, i: (b, 2 * N_HEADS + h))],
        out_specs=pl.BlockSpec((ATT_TQ, HEAD_DIM), lambda b, h, i: (b * nq + i, h)),
        out_shape=jax.ShapeDtypeStruct((batch * seq, WIDTH), F32),
        scratch_shapes=[pltpu.VMEM((DIL_NOFF, ATT_TQ, ATT_TK), F32)] + _softmax_scratch(),
        compiler_params=_params(3),
        name="dilated_attention",
    )(slopes, qkv, qkv, qkv)


def kernel(x, positions, ffn1_norm, ffn1_w_gate, ffn1_w_up, ffn1_w_down, mix_norm, w_in, g_cq, w_uq,
           g_ckv, w_ukv, g_out_a, g_out_b, w_o, ffn2_norm, ffn2_w_gate, ffn2_w_up, ffn2_w_down,
           final_norm):
    batch, seq, d = x.shape
    m = batch * seq
    depth = w_in.shape[0]

    inv_freq = 1.0 / (ROPE_THETA ** (jnp.arange(0, ROPE_DIM, 2, dtype=F32) / ROPE_DIM))
    ang = positions.astype(F32)[..., None] * inv_freq
    cos = jnp.cos(ang).reshape(m, ROPE_DIM // 2)
    sin = jnp.sin(ang).reshape(m, ROPE_DIM // 2)
    zeros = jnp.zeros((m, 128 - ROPE_DIM), F32)
    cos_p = jnp.concatenate([cos, cos, zeros], axis=-1)
    sin_p = jnp.concatenate([-sin, sin, zeros], axis=-1)

    slopes = jnp.asarray(
        np.array([2.0 ** (-8.0 * (i + 1) / N_HEADS) for i in range(N_HEADS)], dtype=np.float32))
    scale_a = HEAD_DIM ** -0.5 * LOG2E
    scale_b = (HEAD_DIM + ROPE_DIM) ** -0.5 * LOG2E

    w_in_t = jnp.swapaxes(w_in, 1, 2)

    h = x.reshape(m, d)
    for l in range(depth):
        n = rmsnorm(h, ffn1_norm[l], BF16)
        hid, w_down = ffn_up(n, ffn1_w_gate, ffn1_w_up, ffn1_w_down, l)
        h = matmul_ksplit_res(hid, w_down, None, h, 0.5)

        n = rmsnorm(h, mix_norm[l], BF16)
        qkv_a = matmul_fullk_t(n, w_in_t, l, out_dtype=BF16, n_cols=3 * WIDTH, scale=scale_a,
                               n_scaled=WIDTH // 512)
        w_c_t = jnp.concatenate(
            [w_in_t[l, 3 * WIDTH:, :], jnp.zeros((128 - ROPE_DIM, d), F32)], axis=0)
        cq_n, ckv_n, k_r = proj_c(n, w_c_t, g_cq[l], g_ckv[l], cos_p, sin_p)

        out_a = dilated_attention(qkv_a, slopes, batch, seq)

        wq = w_uq[l].reshape(Q_LORA, N_HEADS, HEAD_DIM + ROPE_DIM)
        wq_rope = jnp.pad(wq[:, :, HEAD_DIM:], ((0, 0), (0, 0), (0, 128 - ROPE_DIM)))
        w_q = jnp.concatenate([wq[:, :, :HEAD_DIM].reshape(Q_LORA, WIDTH),
                               wq_rope.reshape(Q_LORA, N_HEADS * 128)], axis=-1)
        q_all = q_up(cq_n, w_q, cos_p, sin_p, scale_b)
        kv = matmul_fullk(ckv_n, w_ukv, l, out_dtype=BF16, tn=1024)
        out_b = mla_attention(q_all, kv, k_r, batch, seq)

        mixed = outnorm(out_a, out_b, g_out_a[l], g_out_b[l])
        h = matmul_fullk(mixed, w_o, l, out_dtype=F32, residual=h, scale=1.0, tm=1024, tn=512,
                         single_buffer=False)

        n = rmsnorm(h, ffn2_norm[l], BF16)
        hid, w_down = ffn_up(n, ffn2_w_gate, ffn2_w_up, ffn2_w_down, l)
        h = matmul_ksplit_res(hid, w_down, None, h, 0.5)

    return rmsnorm(h, final_norm, F32).reshape(batch, seq, d)
```

```python
import functools
import math

import numpy as np
import jax
import jax.numpy as jnp
from jax import lax
from jax.experimental import pallas as pl
from jax.experimental.pallas import tpu as pltpu

F32 = jnp.float32
BF16 = jnp.bfloat16

HEAD_DIM = 128
N_HEADS = 16
WIDTH = N_HEADS * HEAD_DIM
DIL_CONFIGS = ((128, 1), (512, 4), (2048, 16))
ROPE_DIM = 64
Q_LORA = 1024
KV_LORA = 512
ROPE_THETA = 10000.0
EPS = 1e-6
MASK = -1e30
LOG2E = math.log2(math.e)

VMEM_LIMIT = 56 * 1024 * 1024
ATT_TQ = 512
ATT_TK = 512
DIL_MAX_DIST = max(w for w, _ in DIL_CONFIGS)
DIL_NOFF = DIL_MAX_DIST // ATT_TK + 1


def _params(n_axes):
    return pltpu.CompilerParams(
        dimension_semantics=("arbitrary",) * n_axes, vmem_limit_bytes=VMEM_LIMIT)


def _rmsnorm_kernel(x_ref, g_ref, o_ref):
    x = x_ref[...]
    ms = jnp.mean(x * x, axis=-1, keepdims=True)
    o_ref[...] = ((x * lax.rsqrt(ms + EPS)) * g_ref[...]).astype(o_ref.dtype)


def rmsnorm(x, g, out_dtype, tm=256):
    m, d = x.shape
    return pl.pallas_call(
        _rmsnorm_kernel,
        grid=(m // tm,),
        in_specs=[pl.BlockSpec((tm, d), lambda i: (i, 0)),
                  pl.BlockSpec((1, d), lambda i: (0, 0))],
        out_specs=pl.BlockSpec((tm, d), lambda i: (i, 0)),
        out_shape=jax.ShapeDtypeStruct((m, d), out_dtype),
        compiler_params=_params(1),
        name="rmsnorm",
    )(x, g.reshape(1, d))


def _outnorm_kernel(a_ref, b_ref, ga_ref, gb_ref, o_ref):
    def norm(x, g):
        ms = jnp.mean(x * x, axis=-1, keepdims=True)
        return ((x * lax.rsqrt(ms + EPS)) * g).astype(o_ref.dtype)
    o_ref[:, :WIDTH] = norm(a_ref[...], ga_ref[...])
    o_ref[:, WIDTH:] = norm(b_ref[...], gb_ref[...])


def outnorm(out_a, out_b, g_a, g_b, tm=256):
    m = out_a.shape[0]
    return pl.pallas_call(
        _outnorm_kernel,
        grid=(m // tm,),
        in_specs=[pl.BlockSpec((tm, WIDTH), lambda i: (i, 0)),
                  pl.BlockSpec((tm, WIDTH), lambda i: (i, 0)),
                  pl.BlockSpec((1, WIDTH), lambda i: (0, 0)),
                  pl.BlockSpec((1, WIDTH), lambda i: (0, 0))],
        out_specs=pl.BlockSpec((tm, 2 * WIDTH), lambda i: (i, 0)),
        out_shape=jax.ShapeDtypeStruct((m, 2 * WIDTH), BF16),
        compiler_params=_params(1),
        name="outnorm",
    )(out_a, out_b, g_a.reshape(1, WIDTH), g_b.reshape(1, WIDTH))


def _dot(a, w):
    return jnp.dot(a, w.astype(BF16), preferred_element_type=F32)


def _dot_t(a, w_t):
    return lax.dot_general(a, w_t.astype(BF16), (((1,), (1,)), ((), ())), preferred_element_type=F32)


def _w_spec(w, layer, block, index_map):
    if w.ndim == 3:
        return pl.BlockSpec((None,) + block, lambda *g: (layer,) + index_map(*g))
    return pl.BlockSpec(block, index_map)


def _resident_spec(tm, kdim, single_buffer=True):
    if single_buffer:
        return pl.BlockSpec((tm, kdim), lambda i, j: (i, 0), pipeline_mode=pl.Buffered(1))
    return pl.BlockSpec((tm, kdim), lambda i, j: (i, 0))


def _mm_fullk_kernel(a_ref, w_ref, o_ref, *, scale, n_scaled):
    s = jnp.where(pl.program_id(1) < n_scaled, jnp.float32(scale), jnp.float32(1.0))
    o_ref[...] = (_dot(a_ref[...], w_ref[...]) * s).astype(o_ref.dtype)


def _mm_fullk_res_kernel(a_ref, w_ref, r_ref, o_ref, *, scale):
    o_ref[...] = r_ref[...] + scale * _dot(a_ref[...], w_ref[...])


def _mm_fullk_t_kernel(a_ref, w_ref, o_ref, *, scale, n_scaled):
    s = jnp.where(pl.program_id(1) < n_scaled, jnp.float32(scale), jnp.float32(1.0))
    o_ref[...] = (_dot_t(a_ref[...], w_ref[...]) * s).astype(o_ref.dtype)


def matmul_fullk_t(a, w_t, layer, *, out_dtype, n_cols, scale=1.0, n_scaled=0, tm=2048, tn=512):
    m, kdim = a.shape
    tm = min(tm, m)
    return pl.pallas_call(
        functools.partial(_mm_fullk_t_kernel, scale=scale, n_scaled=n_scaled),
        grid=(m // tm, n_cols // tn),
        in_specs=[_resident_spec(tm, kdim),
                  pl.BlockSpec((None, tn, kdim), lambda i, j: (layer, j, 0))],
        out_specs=pl.BlockSpec((tm, tn), lambda i, j: (i, j)),
        out_shape=jax.ShapeDtypeStruct((m, n_cols), out_dtype),
        compiler_params=_params(2),
        name="matmul_fullk_t",
    )(a, w_t)


def matmul_fullk(a, w, layer, *, out_dtype, n_cols=None, scale=1.0, n_scaled=0, residual=None,
                 tm=2048, tn=256, single_buffer=True):
    m, kdim = a.shape
    n = w.shape[-1] if n_cols is None else n_cols
    tn = min(tn, n)
    tm = min(tm, m)
    a_spec = _resident_spec(tm, kdim, single_buffer)
    w_spec = _w_spec(w, layer, (kdim, tn), lambda i, j: (0, j))
    o_spec = pl.BlockSpec((tm, tn), lambda i, j: (i, j))
    if residual is None:
        body = functools.partial(_mm_fullk_kernel, scale=scale, n_scaled=n_scaled)
        in_specs, args = [a_spec, w_spec], (a, w)
    else:
        body = functools.partial(_mm_fullk_res_kernel, scale=scale)
        in_specs, args = [a_spec, w_spec, o_spec], (a, w, residual)
    return pl.pallas_call(
        body,
        grid=(m // tm, n // tn),
        in_specs=in_specs,
        out_specs=o_spec,
        out_shape=jax.ShapeDtypeStruct((m, n), out_dtype),
        compiler_params=_params(2),
        name="matmul_fullk_res" if residual is not None else "matmul_fullk",
    )(*args)


def _ffn_up_kernel(a_ref, wg_ref, wu_ref, wd_ref, o_ref, wd_bf16_ref, *, n_chunks):
    step = pl.program_id(0) * pl.num_programs(1) + pl.program_id(1)

    @pl.when(step < n_chunks)
    def _():
        wd_bf16_ref[...] = wd_ref[...].astype(BF16)

    a = a_ref[...]
    g = _dot(a, wg_ref[...])
    u = _dot(a, wu_ref[...])
    o_ref[...] = (g * jax.nn.sigmoid(g) * u).astype(o_ref.dtype)


def ffn_up(a, w_gate, w_up, w_down, layer, tm=2048, tn=256):
    m, kdim = a.shape
    n = w_gate.shape[-1]
    tm = min(tm, m)
    n_j = n // tn
    n_steps = (m // tm) * n_j
    chunk = next(c for c in range(128, n + 1, 128) if n % c == 0 and n // c <= n_steps)
    n_chunks = n // chunk
    d_out = w_down.shape[-1]
    chunk_of = lambda i, j: jnp.minimum(i * n_j + j, n_chunks - 1)
    w_spec = _w_spec(w_gate, layer, (kdim, tn), lambda i, j: (0, j))
    return pl.pallas_call(
        functools.partial(_ffn_up_kernel, n_chunks=n_chunks),
        grid=(m // tm, n_j),
        in_specs=[_resident_spec(tm, kdim), w_spec, w_spec,
                  pl.BlockSpec((None, chunk, d_out), lambda i, j: (layer, chunk_of(i, j), 0))],
        out_specs=[pl.BlockSpec((tm, tn), lambda i, j: (i, j)),
                   pl.BlockSpec((chunk, d_out), lambda i, j: (chunk_of(i, j), 0))],
        out_shape=[jax.ShapeDtypeStruct((m, n), BF16),
                   jax.ShapeDtypeStruct((n, d_out), BF16)],
        compiler_params=_params(2),
        name="ffn_up",
    )(a, w_gate, w_up, w_down)


def _mm_ksplit_res_kernel(a_ref, w_ref, r_ref, o_ref, acc_ref, *, nk, k_rem, scale):
    k = pl.program_id(2)

    @pl.when(k == 0)
    def _():
        acc_ref[...] = _dot(a_ref[...], w_ref[...])

    @pl.when(jnp.logical_and(k > 0, k < nk - 1))
    def _():
        acc_ref[...] += _dot(a_ref[...], w_ref[...])

    @pl.when(k == nk - 1)
    def _():
        tail = _dot(a_ref[:, :k_rem], w_ref[:k_rem, :])
        o_ref[...] = r_ref[...] + scale * (acc_ref[...] + tail)


def matmul_ksplit_res(a, w, layer, residual, scale, tm=1024, tn=1024, tk=2816):
    m, kdim = a.shape
    n = w.shape[-1]
    tn = min(tn, n)
    tk = min(tk, kdim)
    nk = pl.cdiv(kdim, tk)
    assert nk >= 2
    k_rem = kdim - (nk - 1) * tk
    o_spec = pl.BlockSpec((tm, tn), lambda j, i, k: (i, j))
    return pl.pallas_call(
        functools.partial(_mm_ksplit_res_kernel, nk=nk, k_rem=k_rem, scale=scale),
        grid=(n // tn, m // tm, nk),
        in_specs=[pl.BlockSpec((tm, tk), lambda j, i, k: (i, k)),
                  _w_spec(w, layer, (tk, tn), lambda j, i, k: (k, j)),
                  o_spec],
        out_specs=o_spec,
        out_shape=jax.ShapeDtypeStruct((m, n), F32),
        scratch_shapes=[pltpu.VMEM((tm, tn), F32)],
        compiler_params=_params(3),
        name="matmul_ksplit_res",
    )(a, w, residual)


def _rope_slab(x, cos_p, sin_p):
    lane = lax.broadcasted_iota(jnp.int32, x.shape, 1)
    half = ROPE_DIM // 2
    swapped = jnp.where(lane < half, pltpu.roll(x, 128 - half, 1), pltpu.roll(x, half, 1))
    return x * cos_p + swapped * sin_p


def _proj_c_kernel(a_ref, w_ref, gq_ref, gkv_ref, cos_ref, sin_ref,
                   cq_ref, ckv_ref, kr_ref, acc_ref, *, nk):
    k = pl.program_id(1)

    @pl.when(k == 0)
    def _():
        acc_ref[...] = _dot_t(a_ref[...], w_ref[...])

    @pl.when(k > 0)
    def _():
        acc_ref[...] += _dot_t(a_ref[...], w_ref[...])

    @pl.when(k == nk - 1)
    def _():
        def norm(x, g):
            ms = jnp.mean(x * x, axis=-1, keepdims=True)
            return (x * lax.rsqrt(ms + EPS)) * g
        cq_ref[...] = norm(acc_ref[:, :Q_LORA], gq_ref[...]).astype(cq_ref.dtype)
        ckv_ref[...] = norm(acc_ref[:, Q_LORA:Q_LORA + KV_LORA], gkv_ref[...]).astype(ckv_ref.dtype)
        kr = acc_ref[:, Q_LORA + KV_LORA:]
        kr_ref[...] = _rope_slab(kr, cos_ref[...], sin_ref[...]).astype(kr_ref.dtype)


def proj_c(a, w_c_t, g_cq, g_ckv, cos_p, sin_p, tm=1024, tk=1024):
    m, kdim = a.shape
    n = w_c_t.shape[0]
    tm = min(tm, m)
    tk = min(tk, kdim)
    nk = kdim // tk
    row = lambda width: pl.BlockSpec((tm, width), lambda i, k: (i, 0))
    vec = lambda width: pl.BlockSpec((1, width), lambda i, k: (0, 0))
    return pl.pallas_call(
        functools.partial(_proj_c_kernel, nk=nk),
        grid=(m // tm, nk),
        in_specs=[pl.BlockSpec((tm, tk), lambda i, k: (i, k)),
                  pl.BlockSpec((n, tk), lambda i, k: (0, k)),
                  vec(Q_LORA), vec(KV_LORA), row(128), row(128)],
        out_specs=[row(Q_LORA), row(KV_LORA), row(128)],
        out_shape=[jax.ShapeDtypeStruct((m, Q_LORA), BF16),
                   jax.ShapeDtypeStruct((m, KV_LORA), BF16),
                   jax.ShapeDtypeStruct((m, 128), BF16)],
        scratch_shapes=[pltpu.VMEM((tm, n), F32)],
        compiler_params=_params(2),
        name="proj_c",
    )(a, w_c_t, g_cq.reshape(1, Q_LORA), g_ckv.reshape(1, KV_LORA), cos_p, sin_p)


def _q_up_kernel(a_ref, w_ref, cos_ref, sin_ref, o_ref, *, scale, n_nope_tiles):
    x = _dot(a_ref[...], w_ref[...])
    j = pl.program_id(1)

    @pl.when(j < n_nope_tiles)
    def _():
        o_ref[...] = (x * scale).astype(o_ref.dtype)

    @pl.when(j >= n_nope_tiles)
    def _():
        cos_p = cos_ref[...]
        sin_p = sin_ref[...]
        for s in range(x.shape[1] // 128):
            slab = x[:, s * 128:(s + 1) * 128]
            o_ref[:, s * 128:(s + 1) * 128] = (_rope_slab(slab, cos_p, sin_p) * scale).astype(o_ref.dtype)


def q_up(a, w_q, cos_p, sin_p, scale, tm=2048, tn=1024):
    m, kdim = a.shape
    tm = min(tm, m)
    n = w_q.shape[1]
    return pl.pallas_call(
        functools.partial(_q_up_kernel, scale=scale, n_nope_tiles=WIDTH // tn),
        grid=(m // tm, n // tn),
        in_specs=[pl.BlockSpec((tm, kdim), lambda i, j: (i, 0)),
                  pl.BlockSpec((kdim, tn), lambda i, j: (0, j)),
                  pl.BlockSpec((tm, 128), lambda i, j: (i, 0)),
                  pl.BlockSpec((tm, 128), lambda i, j: (i, 0))],
        out_specs=pl.BlockSpec((tm, tn), lambda i, j: (i, j)),
        out_shape=jax.ShapeDtypeStruct((m, n), BF16),
        compiler_params=_params(2),
        name="q_up",
    )(a, w_q, cos_p, sin_p)


def _qk(q, k):
    return lax.dot_general(q, k, (((1,), (1,)), ((), ())), preferred_element_type=F32)


def _softmax_update(s, v, m_ref, l_ref, acc_ref):
    m_prev = m_ref[...]
    m_new = jnp.maximum(m_prev, jnp.max(s, axis=-1, keepdims=True))
    alpha = jnp.exp2(m_prev - m_new)
    p = jnp.exp2(s - jnp.concatenate([m_new] * (s.shape[1] // 128), axis=1))
    l_ref[...] = alpha * l_ref[...] + jnp.sum(p, axis=-1, keepdims=True)
    acc_ref[...] = alpha * acc_ref[...] + jnp.dot(p.astype(BF16), v, preferred_element_type=F32)
    m_ref[...] = m_new


def _softmax_reset(m_ref, l_ref, acc_ref):
    m_ref[...] = jnp.full(m_ref.shape, MASK, F32)
    l_ref[...] = jnp.zeros(l_ref.shape, F32)
    acc_ref[...] = jnp.zeros(acc_ref.shape, F32)


def _softmax_scratch(nq):
    return [pltpu.VMEM((nq, ATT_TQ, 128), F32), pltpu.VMEM((nq, ATT_TQ, 128), F32),
            pltpu.VMEM((nq, ATT_TQ, HEAD_DIM), F32)]


def _head_block(seq, col):
    return pl.BlockSpec((seq, HEAD_DIM), col)


def _mla_kernel(qn_ref, qr_ref, kn_ref, kr_ref, v_ref, o_ref, kc_ref, m_ref, l_ref, acc_ref):
    kc_ref[:, :HEAD_DIM] = kn_ref[...]
    kc_ref[:, HEAD_DIM:] = kr_ref[...]
    seq = o_ref.shape[0]
    row = lax.broadcasted_iota(jnp.int32, (ATT_TQ, ATT_TK), 0)
    col = lax.broadcasted_iota(jnp.int32, (ATT_TQ, ATT_TK), 1)
    for qi in range(seq // ATT_TQ):
        rows = pl.ds(qi * ATT_TQ, ATT_TQ)
        q = jnp.concatenate([qn_ref[rows, :], qr_ref[rows, :]], axis=-1)
        state = (m_ref.at[qi], l_ref.at[qi], acc_ref.at[qi])
        _softmax_reset(*state)
        for j in range(qi + 1):
            keys = pl.ds(j * ATT_TK, ATT_TK)
            s = _qk(q, kc_ref[keys, :])
            if j == qi:
                s = jnp.where(row >= col, s, MASK)
            _softmax_update(s, v_ref[keys, :], *state)
        o_ref[rows, :] = acc_ref[qi] / l_ref[qi]


def mla_attention(q_all, kv, k_r, batch, seq):
    return pl.pallas_call(
        _mla_kernel,
        grid=(batch, N_HEADS),
        in_specs=[_head_block(seq, lambda b, h: (b, h)),
                  _head_block(seq, lambda b, h: (b, N_HEADS + h)),
                  _head_block(seq, lambda b, h: (b, 2 * h)),
                  _head_block(seq, lambda b, h: (b, 0)),
                  _head_block(seq, lambda b, h: (b, 2 * h + 1))],
        out_specs=_head_block(seq, lambda b, h: (b, h)),
        out_shape=jax.ShapeDtypeStruct((batch * seq, WIDTH), F32),
        scratch_shapes=[pltpu.VMEM((seq, 2 * HEAD_DIM), BF16)] + _softmax_scratch(seq // ATT_TQ),
        compiler_params=_params(2),
        name="mla_attention",
    )(q_all, q_all, kv, k_r, kv)


def _dilated_bias(off, neg_slope):
    row = lax.broadcasted_iota(jnp.int32, (ATT_TQ, ATT_TK), 0)
    col = lax.broadcasted_iota(jnp.int32, (ATT_TQ, ATT_TK), 1)
    dist = row - col + off * ATT_TK
    mult = jnp.zeros((ATT_TQ, ATT_TK), jnp.int32)
    for window, dilation in DIL_CONFIGS:
        assert dilation & (dilation - 1) == 0
        held = (dist >= 0) & (dist <= window) & ((dist & (dilation - 1)) == 0)
        mult = mult + held.astype(jnp.int32)
    log2_mult = jnp.full((ATT_TQ, ATT_TK), MASK, F32)
    for n in range(1, len(DIL_CONFIGS) + 1):
        log2_mult = jnp.where(mult == n, jnp.float32(math.log2(n)), log2_mult)
    return neg_slope * dist.astype(F32) + log2_mult


def _dilated_kernel(slope_ref, q_ref, k_ref, v_ref, o_ref, bias_ref, m_ref, l_ref, acc_ref):
    neg_slope = -slope_ref[pl.program_id(1)] * LOG2E
    for off in range(DIL_NOFF):
        bias_ref[off] = _dilated_bias(off, neg_slope)
    seq = o_ref.shape[0]
    for qi in range(seq // ATT_TQ):
        rows = pl.ds(qi * ATT_TQ, ATT_TQ)
        q = q_ref[rows, :]
        state = (m_ref.at[qi], l_ref.at[qi], acc_ref.at[qi])
        _softmax_reset(*state)
        for off in range(min(qi, DIL_NOFF - 1), -1, -1):
            keys = pl.ds((qi - off) * ATT_TK, ATT_TK)
            s = _qk(q, k_ref[keys, :]) + bias_ref[off]
            _softmax_update(s, v_ref[keys, :], *state)
        o_ref[rows, :] = acc_ref[qi] / l_ref[qi]


def dilated_attention(qkv, slopes, batch, seq):
    return pl.pallas_call(
        _dilated_kernel,
        grid=(batch, N_HEADS),
        in_specs=[pl.BlockSpec(memory_space=pltpu.SMEM),
                  _head_block(seq, lambda b, h: (b, h)),
                  _head_block(seq, lambda b, h: (b, N_HEADS + h)),
                  _head_block(seq, lambda b, h: (b, 2 * N_HEADS + h))],
        out_specs=_head_block(seq, lambda b, h: (b, h)),
        out_shape=jax.ShapeDtypeStruct((batch * seq, WIDTH), F32),
        scratch_shapes=[pltpu.VMEM((DIL_NOFF, ATT_TQ, ATT_TK), F32)] + _softmax_scratch(seq // ATT_TQ),
        compiler_params=_params(2),
        name="dilated_attention",
    )(slopes, qkv, qkv, qkv)


def kernel(x, positions, ffn1_norm, ffn1_w_gate, ffn1_w_up, ffn1_w_down, mix_norm, w_in, g_cq, w_uq,
           g_ckv, w_ukv, g_out_a, g_out_b, w_o, ffn2_norm, ffn2_w_gate, ffn2_w_up, ffn2_w_down,
           final_norm):
    batch, seq, d = x.shape
    m = batch * seq
    depth = w_in.shape[0]

    inv_freq = 1.0 / (ROPE_THETA ** (jnp.arange(0, ROPE_DIM, 2, dtype=F32) / ROPE_DIM))
    ang = positions.astype(F32)[..., None] * inv_freq
    cos = jnp.cos(ang).reshape(m, ROPE_DIM // 2)
    sin = jnp.sin(ang).reshape(m, ROPE_DIM // 2)
    zeros = jnp.zeros((m, 128 - ROPE_DIM), F32)
    cos_p = jnp.concatenate([cos, cos, zeros], axis=-1)
    sin_p = jnp.concatenate([-sin, sin, zeros], axis=-1)

    slopes = jnp.asarray(
        np.array([2.0 ** (-8.0 * (i + 1) / N_HEADS) for i in range(N_HEADS)], dtype=np.float32))
    scale_a = HEAD_DIM ** -0.5 * LOG2E
    scale_b = (HEAD_DIM + ROPE_DIM) ** -0.5 * LOG2E

    w_in_t = jnp.swapaxes(w_in, 1, 2)

    h = x.reshape(m, d)
    for l in range(depth):
        n = rmsnorm(h, ffn1_norm[l], BF16)
        hid, w_down = ffn_up(n, ffn1_w_gate, ffn1_w_up, ffn1_w_down, l)
        h = matmul_ksplit_res(hid, w_down, None, h, 0.5)

        n = rmsnorm(h, mix_norm[l], BF16)
        qkv_a = matmul_fullk_t(n, w_in_t, l, out_dtype=BF16, n_cols=3 * WIDTH, scale=scale_a,
                               n_scaled=WIDTH // 512)
        w_c_t = jnp.concatenate(
            [w_in_t[l, 3 * WIDTH:, :], jnp.zeros((128 - ROPE_DIM, d), F32)], axis=0)
        cq_n, ckv_n, k_r = proj_c(n, w_c_t, g_cq[l], g_ckv[l], cos_p, sin_p)

        out_a = dilated_attention(qkv_a, slopes, batch, seq)

        wq = w_uq[l].reshape(Q_LORA, N_HEADS, HEAD_DIM + ROPE_DIM)
        wq_rope = jnp.pad(wq[:, :, HEAD_DIM:], ((0, 0), (0, 0), (0, 128 - ROPE_DIM)))
        w_q = jnp.concatenate([wq[:, :, :HEAD_DIM].reshape(Q_LORA, WIDTH),
                               wq_rope.reshape(Q_LORA, N_HEADS * 128)], axis=-1)
        q_all = q_up(cq_n, w_q, cos_p, sin_p, scale_b)
        kv = matmul_fullk(ckv_n, w_ukv, l, out_dtype=BF16, tn=1024)
        out_b = mla_attention(q_all, kv, k_r, batch, seq)

        mixed = outnorm(out_a, out_b, g_out_a[l], g_out_b[l])
        h = matmul_fullk(mixed, w_o, l, out_dtype=F32, residual=h, scale=1.0, tm=1024, tn=512,
                         single_buffer=False)

        n = rmsnorm(h, ffn2_norm[l], BF16)
        hid, w_down = ffn_up(n, ffn2_w_gate, ffn2_w_up, ffn2_w_down, l)
        h = matmul_ksplit_res(hid, w_down, None, h, 0.5)

    return rmsnorm(h, final_norm, F32).reshape(batch, seq, d)
```

```python
import functools
import math

import numpy as np
import jax
import jax.numpy as jnp
from jax import lax
from jax.experimental import pallas as pl
from jax.experimental.pallas import tpu as pltpu

F32 = jnp.float32
BF16 = jnp.bfloat16

HEAD_DIM = 128
N_HEADS = 16
WIDTH = N_HEADS * HEAD_DIM
DIL_CONFIGS = ((128, 1), (512, 4), (2048, 16))
ROPE_DIM = 64
Q_LORA = 1024
KV_LORA = 512
ROPE_THETA = 10000.0
EPS = 1e-6
MASK = -1e30
LOG2E = math.log2(math.e)

VMEM_LIMIT = 56 * 1024 * 1024
ATT_TQ = 512
ATT_TK = 512
DIL_MAX_DIST = max(w for w, _ in DIL_CONFIGS)
DIL_NOFF = DIL_MAX_DIST // ATT_TK + 1


def _params(n_axes):
    return pltpu.CompilerParams(
        dimension_semantics=("arbitrary",) * n_axes, vmem_limit_bytes=VMEM_LIMIT)


def _rmsnorm_kernel(x_ref, g_ref, o_ref):
    x = x_ref[...]
    ms = jnp.mean(x * x, axis=-1, keepdims=True)
    o_ref[...] = ((x * lax.rsqrt(ms + EPS)) * g_ref[...]).astype(o_ref.dtype)


def rmsnorm(x, g, out_dtype, tm=256):
    m, d = x.shape
    return pl.pallas_call(
        _rmsnorm_kernel,
        grid=(m // tm,),
        in_specs=[pl.BlockSpec((tm, d), lambda i: (i, 0)),
                  pl.BlockSpec((1, d), lambda i: (0, 0))],
        out_specs=pl.BlockSpec((tm, d), lambda i: (i, 0)),
        out_shape=jax.ShapeDtypeStruct((m, d), out_dtype),
        compiler_params=_params(1),
        name="rmsnorm",
    )(x, g.reshape(1, d))


def _rstd(ssq_ref, dim):
    return lax.rsqrt(ssq_ref[...] * (1.0 / dim) + EPS)


def _outnorm_kernel(a_ref, b_ref, ga_ref, gb_ref, o_ref):
    def norm(x, g):
        ms = jnp.mean(x * x, axis=-1, keepdims=True)
        return ((x * lax.rsqrt(ms + EPS)) * g).astype(o_ref.dtype)
    o_ref[:, :WIDTH] = norm(a_ref[...], ga_ref[...])
    o_ref[:, WIDTH:] = norm(b_ref[...], gb_ref[...])


def outnorm(out_a, out_b, g_a, g_b, tm=256):
    m = out_a.shape[0]
    return pl.pallas_call(
        _outnorm_kernel,
        grid=(m // tm,),
        in_specs=[pl.BlockSpec((tm, WIDTH), lambda i: (i, 0)),
                  pl.BlockSpec((tm, WIDTH), lambda i: (i, 0)),
                  pl.BlockSpec((1, WIDTH), lambda i: (0, 0)),
                  pl.BlockSpec((1, WIDTH), lambda i: (0, 0))],
        out_specs=pl.BlockSpec((tm, 2 * WIDTH), lambda i: (i, 0)),
        out_shape=jax.ShapeDtypeStruct((m, 2 * WIDTH), BF16),
        compiler_params=_params(1),
        name="outnorm",
    )(out_a, out_b, g_a.reshape(1, WIDTH), g_b.reshape(1, WIDTH))


def _dot(a, w):
    return jnp.dot(a, w.astype(BF16), preferred_element_type=F32)


def _dot_t(a, w_t):
    return lax.dot_general(a, w_t.astype(BF16), (((1,), (1,)), ((), ())), preferred_element_type=F32)


def _w_spec(w, layer, block, index_map):
    if w.ndim == 3:
        return pl.BlockSpec((None,) + block, lambda *g: (layer,) + index_map(*g))
    return pl.BlockSpec(block, index_map)


def _resident_spec(tm, kdim, single_buffer=True):
    if single_buffer:
        return pl.BlockSpec((tm, kdim), lambda i, j: (i, 0), pipeline_mode=pl.Buffered(1))
    return pl.BlockSpec((tm, kdim), lambda i, j: (i, 0))


def _mm_fullk_kernel(a_ref, w_ref, o_ref, *, scale, n_scaled):
    s = jnp.where(pl.program_id(1) < n_scaled, jnp.float32(scale), jnp.float32(1.0))
    o_ref[...] = (_dot(a_ref[...], w_ref[...]) * s).astype(o_ref.dtype)


def _mm_fullk_res_kernel(a_ref, w_ref, r_ref, o_ref, *, scale):
    o_ref[...] = r_ref[...] + scale * _dot(a_ref[...], w_ref[...])


def _mm_fullk_t_kernel(a_ref, ssq_ref, w_ref, o_ref, *, scale, n_scaled):
    s = jnp.where(pl.program_id(1) < n_scaled, jnp.float32(scale), jnp.float32(1.0))
    r = _rstd(ssq_ref, a_ref.shape[1]) * s
    o_ref[...] = (_dot_t(a_ref[...], w_ref[...]) * r).astype(o_ref.dtype)


def matmul_fullk_t(a, ssq, w_t, layer, *, out_dtype, n_cols, scale=1.0, n_scaled=0, tm=2048, tn=512):
    m, kdim = a.shape
    tm = min(tm, m)
    return pl.pallas_call(
        functools.partial(_mm_fullk_t_kernel, scale=scale, n_scaled=n_scaled),
        grid=(m // tm, n_cols // tn),
        in_specs=[_resident_spec(tm, kdim),
                  pl.BlockSpec((tm, 1), lambda i, j: (i, 0)),
                  pl.BlockSpec((None, tn, kdim), lambda i, j: (layer, j, 0))],
        out_specs=pl.BlockSpec((tm, tn), lambda i, j: (i, j)),
        out_shape=jax.ShapeDtypeStruct((m, n_cols), out_dtype),
        compiler_params=_params(2),
        name="matmul_fullk_t",
    )(a, ssq, w_t)


def matmul_fullk(a, w, layer, *, out_dtype, n_cols=None, scale=1.0, n_scaled=0, residual=None,
                 tm=2048, tn=256, single_buffer=True):
    m, kdim = a.shape
    n = w.shape[-1] if n_cols is None else n_cols
    tn = min(tn, n)
    tm = min(tm, m)
    a_spec = _resident_spec(tm, kdim, single_buffer)
    w_spec = _w_spec(w, layer, (kdim, tn), lambda i, j: (0, j))
    o_spec = pl.BlockSpec((tm, tn), lambda i, j: (i, j))
    if residual is None:
        body = functools.partial(_mm_fullk_kernel, scale=scale, n_scaled=n_scaled)
        in_specs, args = [a_spec, w_spec], (a, w)
    else:
        body = functools.partial(_mm_fullk_res_kernel, scale=scale)
        in_specs, args = [a_spec, w_spec, o_spec], (a, w, residual)
    return pl.pallas_call(
        body,
        grid=(m // tm, n // tn),
        in_specs=in_specs,
        out_specs=o_spec,
        out_shape=jax.ShapeDtypeStruct((m, n), out_dtype),
        compiler_params=_params(2),
        name="matmul_fullk_res" if residual is not None else "matmul_fullk",
    )(*args)


def _ffn_up_kernel(a_ref, wg_ref, wu_ref, wd_ref, o_ref, wd_bf16_ref, *, n_chunks):
    step = pl.program_id(0) * pl.num_programs(1) + pl.program_id(1)

    @pl.when(step < n_chunks)
    def _():
        wd_bf16_ref[...] = wd_ref[...].astype(BF16)

    a = a_ref[...]
    g = _dot(a, wg_ref[...])
    u = _dot(a, wu_ref[...])
    o_ref[...] = (g * jax.nn.sigmoid(g) * u).astype(o_ref.dtype)


def ffn_up(a, w_gate, w_up, w_down, layer, tm=2048, tn=256):
    m, kdim = a.shape
    n = w_gate.shape[-1]
    tm = min(tm, m)
    n_j = n // tn
    n_steps = (m // tm) * n_j
    chunk = next(c for c in range(128, n + 1, 128) if n % c == 0 and n // c <= n_steps)
    n_chunks = n // chunk
    d_out = w_down.shape[-1]
    chunk_of = lambda i, j: jnp.minimum(i * n_j + j, n_chunks - 1)
    w_spec = _w_spec(w_gate, layer, (kdim, tn), lambda i, j: (0, j))
    return pl.pallas_call(
        functools.partial(_ffn_up_kernel, n_chunks=n_chunks),
        grid=(m // tm, n_j),
        in_specs=[_resident_spec(tm, kdim), w_spec, w_spec,
                  pl.BlockSpec((None, chunk, d_out), lambda i, j: (layer, chunk_of(i, j), 0))],
        out_specs=[pl.BlockSpec((tm, tn), lambda i, j: (i, j)),
                   pl.BlockSpec((chunk, d_out), lambda i, j: (chunk_of(i, j), 0))],
        out_shape=[jax.ShapeDtypeStruct((m, n), BF16),
                   jax.ShapeDtypeStruct((n, d_out), BF16)],
        compiler_params=_params(2),
        name="ffn_up",
    )(a, w_gate, w_up, w_down)


def _mm_ksplit_res_kernel(a_ref, w_ref, r_ref, *rest, nk, k_rem, scale, emit):
    if emit:
        g_ref, o_ref, og_ref, ssq_ref, acc_ref = rest
    else:
        o_ref, acc_ref = rest
    k = pl.program_id(2)

    @pl.when(k == 0)
    def _():
        acc_ref[...] = _dot(a_ref[...], w_ref[...])

    @pl.when(jnp.logical_and(k > 0, k < nk - 1))
    def _():
        acc_ref[...] += _dot(a_ref[...], w_ref[...])

    @pl.when(k == nk - 1)
    def _():
        tail = _dot(a_ref[:, :k_rem], w_ref[:k_rem, :])
        o = r_ref[...] + scale * (acc_ref[...] + tail)
        o_ref[...] = o
        if emit:
            og_ref[...] = (o * g_ref[...]).astype(og_ref.dtype)
            part = jnp.sum(o * o, axis=-1, keepdims=True)
            j = pl.program_id(1)

            @pl.when(j == 0)
            def _():
                ssq_ref[...] = part

            @pl.when(j > 0)
            def _():
                ssq_ref[...] += part


def matmul_ksplit_res(a, w, residual, scale, g_next=None, tm=1024, tn=1024, tk=2816):
    m, kdim = a.shape
    n = w.shape[-1]
    tn = min(tn, n)
    tk = min(tk, kdim)
    nk = pl.cdiv(kdim, tk)
    assert nk >= 2
    k_rem = kdim - (nk - 1) * tk
    emit = g_next is not None
    o_spec = pl.BlockSpec((tm, tn), lambda i, j, k: (i, j))
    in_specs = [pl.BlockSpec((tm, tk), lambda i, j, k: (i, k)),
                pl.BlockSpec((tk, tn), lambda i, j, k: (k, j)),
                o_spec]
    args = (a, w, residual)
    out_specs, out_shape = o_spec, jax.ShapeDtypeStruct((m, n), F32)
    if emit:
        in_specs.append(pl.BlockSpec((1, tn), lambda i, j, k: (0, j)))
        args += (g_next.reshape(1, n),)
        out_specs = [o_spec, o_spec, pl.BlockSpec((tm, 1), lambda i, j, k: (i, 0))]
        out_shape = [out_shape, jax.ShapeDtypeStruct((m, n), BF16), jax.ShapeDtypeStruct((m, 1), F32)]
    return pl.pallas_call(
        functools.partial(_mm_ksplit_res_kernel, nk=nk, k_rem=k_rem, scale=scale, emit=emit),
        grid=(m // tm, n // tn, nk),
        in_specs=in_specs,
        out_specs=out_specs,
        out_shape=out_shape,
        scratch_shapes=[pltpu.VMEM((tm, tn), F32)],
        compiler_params=_params(3),
        name="matmul_ksplit_res",
    )(*args)


def _rope_slab(x, cos_p, sin_p):
    lane = lax.broadcasted_iota(jnp.int32, x.shape, 1)
    half = ROPE_DIM // 2
    swapped = jnp.where(lane < half, pltpu.roll(x, 128 - half, 1), pltpu.roll(x, half, 1))
    return x * cos_p + swapped * sin_p


def _proj_c_kernel(a_ref, ssq_ref, w_ref, gq_ref, gkv_ref, cos_ref, sin_ref,
                   cq_ref, ckv_ref, kr_ref, acc_ref, *, nk, dim):
    k = pl.program_id(1)

    @pl.when(k == 0)
    def _():
        acc_ref[...] = _dot_t(a_ref[...], w_ref[...])

    @pl.when(k > 0)
    def _():
        acc_ref[...] += _dot_t(a_ref[...], w_ref[...])

    @pl.when(k == nk - 1)
    def _():
        def norm(x, g):
            ms = jnp.mean(x * x, axis=-1, keepdims=True)
            return (x * lax.rsqrt(ms + EPS)) * g
        r = _rstd(ssq_ref, dim)
        cq_ref[...] = norm(acc_ref[:, :Q_LORA] * r, gq_ref[...]).astype(cq_ref.dtype)
        ckv_ref[...] = norm(acc_ref[:, Q_LORA:Q_LORA + KV_LORA] * r, gkv_ref[...]).astype(ckv_ref.dtype)
        kr = acc_ref[:, Q_LORA + KV_LORA:] * r
        kr_ref[...] = _rope_slab(kr, cos_ref[...], sin_ref[...]).astype(kr_ref.dtype)


def proj_c(a, ssq, w_c_t, g_cq, g_ckv, cos_p, sin_p, tm=1024, tk=1024):
    m, kdim = a.shape
    n = w_c_t.shape[0]
    tm = min(tm, m)
    tk = min(tk, kdim)
    nk = kdim // tk
    row = lambda width: pl.BlockSpec((tm, width), lambda i, k: (i, 0))
    vec = lambda width: pl.BlockSpec((1, width), lambda i, k: (0, 0))
    return pl.pallas_call(
        functools.partial(_proj_c_kernel, nk=nk, dim=kdim),
        grid=(m // tm, nk),
        in_specs=[pl.BlockSpec((tm, tk), lambda i, k: (i, k)),
                  row(1),
                  pl.BlockSpec((n, tk), lambda i, k: (0, k)),
                  vec(Q_LORA), vec(KV_LORA), row(128), row(128)],
        out_specs=[row(Q_LORA), row(KV_LORA), row(128)],
        out_shape=[jax.ShapeDtypeStruct((m, Q_LORA), BF16),
                   jax.ShapeDtypeStruct((m, KV_LORA), BF16),
                   jax.ShapeDtypeStruct((m, 128), BF16)],
        scratch_shapes=[pltpu.VMEM((tm, n), F32)],
        compiler_params=_params(2),
        name="proj_c",
    )(a, ssq, w_c_t, g_cq.reshape(1, Q_LORA), g_ckv.reshape(1, KV_LORA), cos_p, sin_p)


def _q_up_kernel(a_ref, w_ref, cos_ref, sin_ref, o_ref, *, scale, n_nope_tiles):
    x = _dot(a_ref[...], w_ref[...])
    j = pl.program_id(1)

    @pl.when(j < n_nope_tiles)
    def _():
        o_ref[...] = (x * scale).astype(o_ref.dtype)

    @pl.when(j >= n_nope_tiles)
    def _():
        cos_p = cos_ref[...]
        sin_p = sin_ref[...]
        for s in range(x.shape[1] // 128):
            slab = x[:, s * 128:(s + 1) * 128]
            o_ref[:, s * 128:(s + 1) * 128] = (_rope_slab(slab, cos_p, sin_p) * scale).astype(o_ref.dtype)


def q_up(a, w_q, cos_p, sin_p, scale, tm=2048, tn=1024):
    m, kdim = a.shape
    tm = min(tm, m)
    n = w_q.shape[1]
    return pl.pallas_call(
        functools.partial(_q_up_kernel, scale=scale, n_nope_tiles=WIDTH // tn),
        grid=(m // tm, n // tn),
        in_specs=[pl.BlockSpec((tm, kdim), lambda i, j: (i, 0)),
                  pl.BlockSpec((kdim, tn), lambda i, j: (0, j)),
                  pl.BlockSpec((tm, 128), lambda i, j: (i, 0)),
                  pl.BlockSpec((tm, 128), lambda i, j: (i, 0))],
        out_specs=pl.BlockSpec((tm, tn), lambda i, j: (i, j)),
        out_shape=jax.ShapeDtypeStruct((m, n), BF16),
        compiler_params=_params(2),
        name="q_up",
    )(a, w_q, cos_p, sin_p)


def _qk(q, k):
    return lax.dot_general(q, k, (((1,), (1,)), ((), ())), preferred_element_type=F32)


def _softmax_update(s, v, m_ref, l_ref, acc_ref):
    m_prev = m_ref[...]
    m_new = jnp.maximum(m_prev, jnp.max(s, axis=-1, keepdims=True))
    alpha = jnp.exp2(m_prev - m_new)
    p = jnp.exp2(s - jnp.concatenate([m_new] * (s.shape[1] // 128), axis=1))
    l_ref[...] = alpha * l_ref[...] + jnp.sum(p, axis=-1, keepdims=True)
    acc_ref[...] = alpha * acc_ref[...] + jnp.dot(p.astype(BF16), v, preferred_element_type=F32)
    m_ref[...] = m_new


def _softmax_reset(m_ref, l_ref, acc_ref):
    m_ref[...] = jnp.full(m_ref.shape, MASK, F32)
    l_ref[...] = jnp.zeros(l_ref.shape, F32)
    acc_ref[...] = jnp.zeros(acc_ref.shape, F32)


def _softmax_scratch(nq):
    return [pltpu.VMEM((nq, ATT_TQ, 128), F32), pltpu.VMEM((nq, ATT_TQ, 128), F32),
            pltpu.VMEM((nq, ATT_TQ, HEAD_DIM), F32)]


def _head_block(seq, col):
    return pl.BlockSpec((seq, HEAD_DIM), col)


def _mla_kernel(qn_ref, qr_ref, kn_ref, kr_ref, v_ref, o_ref, kc_ref, m_ref, l_ref, acc_ref):
    kc_ref[:, :HEAD_DIM] = kn_ref[...]
    kc_ref[:, HEAD_DIM:] = kr_ref[...]
    seq = o_ref.shape[0]
    row = lax.broadcasted_iota(jnp.int32, (ATT_TQ, ATT_TK), 0)
    col = lax.broadcasted_iota(jnp.int32, (ATT_TQ, ATT_TK), 1)
    for qi in range(seq // ATT_TQ):
        rows = pl.ds(qi * ATT_TQ, ATT_TQ)
        q = jnp.concatenate([qn_ref[rows, :], qr_ref[rows, :]], axis=-1)
        state = (m_ref.at[qi], l_ref.at[qi], acc_ref.at[qi])
        _softmax_reset(*state)
        for j in range(qi + 1):
            keys = pl.ds(j * ATT_TK, ATT_TK)
            s = _qk(q, kc_ref[keys, :])
            if j == qi:
                s = jnp.where(row >= col, s, MASK)
            _softmax_update(s, v_ref[keys, :], *state)
        o_ref[rows, :] = acc_ref[qi] / l_ref[qi]


def mla_attention(q_all, kv, k_r, batch, seq):
    return pl.pallas_call(
        _mla_kernel,
        grid=(batch, N_HEADS),
        in_specs=[_head_block(seq, lambda b, h: (b, h)),
                  _head_block(seq, lambda b, h: (b, N_HEADS + h)),
                  _head_block(seq, lambda b, h: (b, 2 * h)),
                  _head_block(seq, lambda b, h: (b, 0)),
                  _head_block(seq, lambda b, h: (b, 2 * h + 1))],
        out_specs=_head_block(seq, lambda b, h: (b, h)),
        out_shape=jax.ShapeDtypeStruct((batch * seq, WIDTH), F32),
        scratch_shapes=[pltpu.VMEM((seq, 2 * HEAD_DIM), BF16)] + _softmax_scratch(seq // ATT_TQ),
        compiler_params=_params(2),
        name="mla_attention",
    )(q_all, q_all, kv, k_r, kv)


def _dilated_tables(off):
    row = lax.broadcasted_iota(jnp.int32, (ATT_TQ, ATT_TK), 0)
    col = lax.broadcasted_iota(jnp.int32, (ATT_TQ, ATT_TK), 1)
    dist = row - col + off * ATT_TK
    mult = jnp.zeros((ATT_TQ, ATT_TK), jnp.int32)
    for window, dilation in DIL_CONFIGS:
        assert dilation & (dilation - 1) == 0
        held = (dist >= 0) & (dist <= window) & ((dist & (dilation - 1)) == 0)
        mult = mult + held.astype(jnp.int32)
    log2_mult = jnp.full((ATT_TQ, ATT_TK), MASK, F32)
    for n in range(1, len(DIL_CONFIGS) + 1):
        log2_mult = jnp.where(mult == n, jnp.float32(math.log2(n)), log2_mult)
    return dist.astype(F32), log2_mult


def _dilated_kernel(slope_ref, q_ref, k_ref, v_ref, o_ref, dist_ref, lmult_ref, bias_ref,
                    m_ref, l_ref, acc_ref):
    @pl.when(jnp.logical_and(pl.program_id(0) == 0, pl.program_id(1) == 0))
    def _():
        for off in range(DIL_NOFF):
            dist_ref[off], lmult_ref[off] = _dilated_tables(off)

    neg_slope = -slope_ref[pl.program_id(1)] * LOG2E
    for off in range(DIL_NOFF):
        bias_ref[off] = neg_slope * dist_ref[off] + lmult_ref[off]
    seq = o_ref.shape[0]
    for qi in range(seq // ATT_TQ):
        rows = pl.ds(qi * ATT_TQ, ATT_TQ)
        q = q_ref[rows, :]
        state = (m_ref.at[qi], l_ref.at[qi], acc_ref.at[qi])
        _softmax_reset(*state)
        for off in range(min(qi, DIL_NOFF - 1), -1, -1):
            keys = pl.ds((qi - off) * ATT_TK, ATT_TK)
            s = _qk(q, k_ref[keys, :]) + bias_ref[off]
            _softmax_update(s, v_ref[keys, :], *state)
        o_ref[rows, :] = acc_ref[qi] / l_ref[qi]


def dilated_attention(qkv, slopes, batch, seq):
    return pl.pallas_call(
        _dilated_kernel,
        grid=(batch, N_HEADS),
        in_specs=[pl.BlockSpec(memory_space=pltpu.SMEM),
                  _head_block(seq, lambda b, h: (b, h)),
                  _head_block(seq, lambda b, h: (b, N_HEADS + h)),
                  _head_block(seq, lambda b, h: (b, 2 * N_HEADS + h))],
        out_specs=_head_block(seq, lambda b, h: (b, h)),
        out_shape=jax.ShapeDtypeStruct((batch * seq, WIDTH), F32),
        scratch_shapes=[pltpu.VMEM((DIL_NOFF, ATT_TQ, ATT_TK), F32)] * 3 + _softmax_scratch(seq // ATT_TQ),
        compiler_params=_params(2),
        name="dilated_attention",
    )(slopes, qkv, qkv, qkv)


def kernel(x, positions, ffn1_norm, ffn1_w_gate, ffn1_w_up, ffn1_w_down, mix_norm, w_in, g_cq, w_uq,
           g_ckv, w_ukv, g_out_a, g_out_b, w_o, ffn2_norm, ffn2_w_gate, ffn2_w_up, ffn2_w_down,
           final_norm):
    batch, seq, d = x.shape
    m = batch * seq
    depth = w_in.shape[0]

    inv_freq = 1.0 / (ROPE_THETA ** (jnp.arange(0, ROPE_DIM, 2, dtype=F32) / ROPE_DIM))
    ang = positions.astype(F32)[..., None] * inv_freq
    cos = jnp.cos(ang).reshape(m, ROPE_DIM // 2)
    sin = jnp.sin(ang).reshape(m, ROPE_DIM // 2)
    zeros = jnp.zeros((m, 128 - ROPE_DIM), F32)
    cos_p = jnp.concatenate([cos, cos, zeros], axis=-1)
    sin_p = jnp.concatenate([-sin, sin, zeros], axis=-1)

    slopes = jnp.asarray(
        np.array([2.0 ** (-8.0 * (i + 1) / N_HEADS) for i in range(N_HEADS)], dtype=np.float32))
    scale_a = HEAD_DIM ** -0.5 * LOG2E
    scale_b = (HEAD_DIM + ROPE_DIM) ** -0.5 * LOG2E

    w_in_t = jnp.swapaxes(w_in, 1, 2)

    h = x.reshape(m, d)
    for l in range(depth):
        n = rmsnorm(h, ffn1_norm[l], BF16)
        hid, w_down = ffn_up(n, ffn1_w_gate, ffn1_w_up, ffn1_w_down, l)
        h, hg, ssq = matmul_ksplit_res(hid, w_down, h, 0.5, g_next=mix_norm[l])

        qkv_a = matmul_fullk_t(hg, ssq, w_in_t, l, out_dtype=BF16, n_cols=3 * WIDTH, scale=scale_a,
                               n_scaled=WIDTH // 512)
        w_c_t = jnp.concatenate(
            [w_in_t[l, 3 * WIDTH:, :], jnp.zeros((128 - ROPE_DIM, d), F32)], axis=0)
        cq_n, ckv_n, k_r = proj_c(hg, ssq, w_c_t, g_cq[l], g_ckv[l], cos_p, sin_p)

        out_a = dilated_attention(qkv_a, slopes, batch, seq)

        wq = w_uq[l].reshape(Q_LORA, N_HEADS, HEAD_DIM + ROPE_DIM)
        wq_rope = jnp.pad(wq[:, :, HEAD_DIM:], ((0, 0), (0, 0), (0, 128 - ROPE_DIM)))
        w_q = jnp.concatenate([wq[:, :, :HEAD_DIM].reshape(Q_LORA, WIDTH),
                               wq_rope.reshape(Q_LORA, N_HEADS * 128)], axis=-1)
        q_all = q_up(cq_n, w_q, cos_p, sin_p, scale_b)
        kv = matmul_fullk(ckv_n, w_ukv, l, out_dtype=BF16, tn=1024)
        out_b = mla_attention(q_all, kv, k_r, batch, seq)

        mixed = outnorm(out_a, out_b, g_out_a[l], g_out_b[l])
        h = matmul_fullk(mixed, w_o, l, out_dtype=F32, residual=h, scale=1.0, tm=1024, tn=512,
                         single_buffer=False)

        n = rmsnorm(h, ffn2_norm[l], BF16)
        hid, w_down = ffn_up(n, ffn2_w_gate, ffn2_w_up, ffn2_w_down, l)
        h = matmul_ksplit_res(hid, w_down, h, 0.5)

    return rmsnorm(h, final_norm, F32).reshape(batch, seq, d)
```

```python
import functools
import math

import numpy as np
import jax
import jax.numpy as jnp
from jax import lax
from jax.experimental import pallas as pl
from jax.experimental.pallas import tpu as pltpu

F32 = jnp.float32
BF16 = jnp.bfloat16

HEAD_DIM = 128
N_HEADS = 16
WIDTH = N_HEADS * HEAD_DIM
DIL_CONFIGS = ((128, 1), (512, 4), (2048, 16))
ROPE_DIM = 64
Q_LORA = 1024
KV_LORA = 512
ROPE_THETA = 10000.0
EPS = 1e-6
MASK = -1e30
LOG2E = math.log2(math.e)

VMEM_LIMIT = 56 * 1024 * 1024
ATT_TQ = 512
ATT_TK = 512
DIL_MAX_DIST = max(w for w, _ in DIL_CONFIGS)
DIL_NOFF = DIL_MAX_DIST // ATT_TK + 1


def _params(n_axes):
    return pltpu.CompilerParams(
        dimension_semantics=("arbitrary",) * n_axes, vmem_limit_bytes=VMEM_LIMIT)


def _rmsnorm_kernel(x_ref, g_ref, o_ref):
    x = x_ref[...]
    ms = jnp.mean(x * x, axis=-1, keepdims=True)
    o_ref[...] = ((x * lax.rsqrt(ms + EPS)) * g_ref[...]).astype(o_ref.dtype)


def rmsnorm(x, g, out_dtype, tm=256):
    m, d = x.shape
    return pl.pallas_call(
        _rmsnorm_kernel,
        grid=(m // tm,),
        in_specs=[pl.BlockSpec((tm, d), lambda i: (i, 0)),
                  pl.BlockSpec((1, d), lambda i: (0, 0))],
        out_specs=pl.BlockSpec((tm, d), lambda i: (i, 0)),
        out_shape=jax.ShapeDtypeStruct((m, d), out_dtype),
        compiler_params=_params(1),
        name="rmsnorm",
    )(x, g.reshape(1, d))


def _rstd(ssq_ref, dim):
    return lax.rsqrt(ssq_ref[...] * (1.0 / dim) + EPS)


def _dot(a, w):
    return jnp.dot(a, w.astype(BF16), preferred_element_type=F32)


def _dot_t(a, w_t):
    return lax.dot_general(a, w_t.astype(BF16), (((1,), (1,)), ((), ())), preferred_element_type=F32)


def _w_spec(w, layer, block, index_map):
    if w.ndim == 3:
        return pl.BlockSpec((None,) + block, lambda *g: (layer,) + index_map(*g))
    return pl.BlockSpec(block, index_map)


def _resident_spec(tm, kdim):
    return pl.BlockSpec((tm, kdim), lambda i, j: (i, 0), pipeline_mode=pl.Buffered(1))


def _mm_fullk_kernel(a_ref, w_ref, o_ref, *, scale, n_scaled):
    s = jnp.where(pl.program_id(1) < n_scaled, jnp.float32(scale), jnp.float32(1.0))
    o_ref[...] = (_dot(a_ref[...], w_ref[...]) * s).astype(o_ref.dtype)


def _out_proj_kernel(a_ref, b_ref, ssq_a_ref, ssq_b_ref, w_ref, r_ref, o_ref):
    ka = a_ref.shape[1]
    ya = _dot(a_ref[...], w_ref[:ka, :]) * _rstd(ssq_a_ref, ka)
    yb = _dot(b_ref[...], w_ref[ka:, :]) * _rstd(ssq_b_ref, b_ref.shape[1])
    o_ref[...] = r_ref[...] + ya + yb


def out_proj(a, b, ssq_a, ssq_b, w, layer, residual, tm=1024, tn=512):
    m, ka = a.shape
    kb = b.shape[1]
    n = w.shape[-1]
    tn = min(tn, n)
    tm = min(tm, m)
    rows = lambda width: pl.BlockSpec((tm, width), lambda i, j: (i, 0))
    o_spec = pl.BlockSpec((tm, tn), lambda i, j: (i, j))
    return pl.pallas_call(
        _out_proj_kernel,
        grid=(m // tm, n // tn),
        in_specs=[rows(ka), rows(kb), rows(1), rows(1),
                  _w_spec(w, layer, (ka + kb, tn), lambda i, j: (0, j)), o_spec],
        out_specs=o_spec,
        out_shape=jax.ShapeDtypeStruct((m, n), F32),
        compiler_params=_params(2),
        name="out_proj",
    )(a, b, ssq_a, ssq_b, w, residual)


def _mm_fullk_t_kernel(a_ref, ssq_ref, w_ref, o_ref, *, scale, n_scaled):
    s = jnp.where(pl.program_id(1) < n_scaled, jnp.float32(scale), jnp.float32(1.0))
    r = _rstd(ssq_ref, a_ref.shape[1]) * s
    o_ref[...] = (_dot_t(a_ref[...], w_ref[...]) * r).astype(o_ref.dtype)


def matmul_fullk_t(a, ssq, w_t, layer, *, out_dtype, n_cols, scale=1.0, n_scaled=0, tm=2048, tn=512):
    m, kdim = a.shape
    tm = min(tm, m)
    return pl.pallas_call(
        functools.partial(_mm_fullk_t_kernel, scale=scale, n_scaled=n_scaled),
        grid=(m // tm, n_cols // tn),
        in_specs=[_resident_spec(tm, kdim),
                  pl.BlockSpec((tm, 1), lambda i, j: (i, 0)),
                  pl.BlockSpec((None, tn, kdim), lambda i, j: (layer, j, 0))],
        out_specs=pl.BlockSpec((tm, tn), lambda i, j: (i, j)),
        out_shape=jax.ShapeDtypeStruct((m, n_cols), out_dtype),
        compiler_params=_params(2),
        name="matmul_fullk_t",
    )(a, ssq, w_t)


def matmul_fullk(a, w, layer, *, out_dtype, tm=2048, tn=256):
    m, kdim = a.shape
    n = w.shape[-1]
    tn = min(tn, n)
    tm = min(tm, m)
    return pl.pallas_call(
        functools.partial(_mm_fullk_kernel, scale=1.0, n_scaled=0),
        grid=(m // tm, n // tn),
        in_specs=[_resident_spec(tm, kdim), _w_spec(w, layer, (kdim, tn), lambda i, j: (0, j))],
        out_specs=pl.BlockSpec((tm, tn), lambda i, j: (i, j)),
        out_shape=jax.ShapeDtypeStruct((m, n), out_dtype),
        compiler_params=_params(2),
        name="matmul_fullk",
    )(a, w)


def _ffn_up_kernel(a_ref, wg_ref, wu_ref, wd_ref, o_ref, wd_bf16_ref, *, n_chunks):
    step = pl.program_id(0) * pl.num_programs(1) + pl.program_id(1)

    @pl.when(step < n_chunks)
    def _():
        wd_bf16_ref[...] = wd_ref[...].astype(BF16)

    a = a_ref[...]
    g = _dot(a, wg_ref[...])
    u = _dot(a, wu_ref[...])
    o_ref[...] = (g * jax.nn.sigmoid(g) * u).astype(o_ref.dtype)


def ffn_up(a, w_gate, w_up, w_down, layer, tm=2048, tn=256):
    m, kdim = a.shape
    n = w_gate.shape[-1]
    tm = min(tm, m)
    n_j = n // tn
    n_steps = (m // tm) * n_j
    chunk = next(c for c in range(128, n + 1, 128) if n % c == 0 and n // c <= n_steps)
    n_chunks = n // chunk
    d_out = w_down.shape[-1]
    chunk_of = lambda i, j: jnp.minimum(i * n_j + j, n_chunks - 1)
    w_spec = _w_spec(w_gate, layer, (kdim, tn), lambda i, j: (0, j))
    return pl.pallas_call(
        functools.partial(_ffn_up_kernel, n_chunks=n_chunks),
        grid=(m // tm, n_j),
        in_specs=[_resident_spec(tm, kdim), w_spec, w_spec,
                  pl.BlockSpec((None, chunk, d_out), lambda i, j: (layer, chunk_of(i, j), 0))],
        out_specs=[pl.BlockSpec((tm, tn), lambda i, j: (i, j)),
                   pl.BlockSpec((chunk, d_out), lambda i, j: (chunk_of(i, j), 0))],
        out_shape=[jax.ShapeDtypeStruct((m, n), BF16),
                   jax.ShapeDtypeStruct((n, d_out), BF16)],
        compiler_params=_params(2),
        name="ffn_up",
    )(a, w_gate, w_up, w_down)


def _mm_ksplit_res_kernel(a_ref, w_ref, r_ref, *rest, nk, k_rem, scale, emit):
    if emit:
        g_ref, o_ref, og_ref, ssq_ref, acc_ref = rest
    else:
        o_ref, acc_ref = rest
    k = pl.program_id(2)

    @pl.when(k == 0)
    def _():
        acc_ref[...] = _dot(a_ref[...], w_ref[...])

    @pl.when(jnp.logical_and(k > 0, k < nk - 1))
    def _():
        acc_ref[...] += _dot(a_ref[...], w_ref[...])

    @pl.when(k == nk - 1)
    def _():
        tail = _dot(a_ref[:, :k_rem], w_ref[:k_rem, :])
        o = r_ref[...] + scale * (acc_ref[...] + tail)
        o_ref[...] = o
        if emit:
            og_ref[...] = (o * g_ref[...]).astype(og_ref.dtype)
            part = jnp.sum(o * o, axis=-1, keepdims=True)
            j = pl.program_id(1)

            @pl.when(j == 0)
            def _():
                ssq_ref[...] = part

            @pl.when(j > 0)
            def _():
                ssq_ref[...] += part


def matmul_ksplit_res(a, w, residual, scale, g_next=None, tm=1024, tn=1024, tk=2816):
    m, kdim = a.shape
    n = w.shape[-1]
    tn = min(tn, n)
    tk = min(tk, kdim)
    nk = pl.cdiv(kdim, tk)
    assert nk >= 2
    k_rem = kdim - (nk - 1) * tk
    emit = g_next is not None
    o_spec = pl.BlockSpec((tm, tn), lambda i, j, k: (i, j))
    in_specs = [pl.BlockSpec((tm, tk), lambda i, j, k: (i, k)),
                pl.BlockSpec((tk, tn), lambda i, j, k: (k, j)),
                o_spec]
    args = (a, w, residual)
    out_specs, out_shape = o_spec, jax.ShapeDtypeStruct((m, n), F32)
    if emit:
        in_specs.append(pl.BlockSpec((1, tn), lambda i, j, k: (0, j)))
        args += (g_next.reshape(1, n),)
        out_specs = [o_spec, o_spec, pl.BlockSpec((tm, 1), lambda i, j, k: (i, 0))]
        out_shape = [out_shape, jax.ShapeDtypeStruct((m, n), BF16), jax.ShapeDtypeStruct((m, 1), F32)]
    return pl.pallas_call(
        functools.partial(_mm_ksplit_res_kernel, nk=nk, k_rem=k_rem, scale=scale, emit=emit),
        grid=(m // tm, n // tn, nk),
        in_specs=in_specs,
        out_specs=out_specs,
        out_shape=out_shape,
        scratch_shapes=[pltpu.VMEM((tm, tn), F32)],
        compiler_params=_params(3),
        name="matmul_ksplit_res",
    )(*args)


def _rope_slab(x, cos_p, sin_p):
    lane = lax.broadcasted_iota(jnp.int32, x.shape, 1)
    half = ROPE_DIM // 2
    swapped = jnp.where(lane < half, pltpu.roll(x, 128 - half, 1), pltpu.roll(x, half, 1))
    return x * cos_p + swapped * sin_p


def _proj_c_kernel(a_ref, ssq_ref, w_ref, gq_ref, gkv_ref, cos_ref, sin_ref,
                   cq_ref, ckv_ref, kr_ref, acc_ref, *, nk, dim):
    k = pl.program_id(1)

    @pl.when(k == 0)
    def _():
        acc_ref[...] = _dot_t(a_ref[...], w_ref[...])

    @pl.when(k > 0)
    def _():
        acc_ref[...] += _dot_t(a_ref[...], w_ref[...])

    @pl.when(k == nk - 1)
    def _():
        def norm(x, g):
            ms = jnp.mean(x * x, axis=-1, keepdims=True)
            return (x * lax.rsqrt(ms + EPS)) * g
        r = _rstd(ssq_ref, dim)
        cq_ref[...] = norm(acc_ref[:, :Q_LORA] * r, gq_ref[...]).astype(cq_ref.dtype)
        ckv_ref[...] = norm(acc_ref[:, Q_LORA:Q_LORA + KV_LORA] * r, gkv_ref[...]).astype(ckv_ref.dtype)
        kr = acc_ref[:, Q_LORA + KV_LORA:] * r
        kr_ref[...] = _rope_slab(kr, cos_ref[...], sin_ref[...]).astype(kr_ref.dtype)


def proj_c(a, ssq, w_c_t, g_cq, g_ckv, cos_p, sin_p, tm=1024, tk=1024):
    m, kdim = a.shape
    n = w_c_t.shape[0]
    tm = min(tm, m)
    tk = min(tk, kdim)
    nk = kdim // tk
    row = lambda width: pl.BlockSpec((tm, width), lambda i, k: (i, 0))
    vec = lambda width: pl.BlockSpec((1, width), lambda i, k: (0, 0))
    return pl.pallas_call(
        functools.partial(_proj_c_kernel, nk=nk, dim=kdim),
        grid=(m // tm, nk),
        in_specs=[pl.BlockSpec((tm, tk), lambda i, k: (i, k)),
                  row(1),
                  pl.BlockSpec((n, tk), lambda i, k: (0, k)),
                  vec(Q_LORA), vec(KV_LORA), row(128), row(128)],
        out_specs=[row(Q_LORA), row(KV_LORA), row(128)],
        out_shape=[jax.ShapeDtypeStruct((m, Q_LORA), BF16),
                   jax.ShapeDtypeStruct((m, KV_LORA), BF16),
                   jax.ShapeDtypeStruct((m, 128), BF16)],
        scratch_shapes=[pltpu.VMEM((tm, n), F32)],
        compiler_params=_params(2),
        name="proj_c",
    )(a, ssq, w_c_t, g_cq.reshape(1, Q_LORA), g_ckv.reshape(1, KV_LORA), cos_p, sin_p)


def _q_up_kernel(a_ref, w_ref, cos_ref, sin_ref, o_ref, *, scale, n_nope_tiles):
    x = _dot(a_ref[...], w_ref[...])
    j = pl.program_id(1)

    @pl.when(j < n_nope_tiles)
    def _():
        o_ref[...] = (x * scale).astype(o_ref.dtype)

    @pl.when(j >= n_nope_tiles)
    def _():
        cos_p = cos_ref[...]
        sin_p = sin_ref[...]
        for s in range(x.shape[1] // 128):
            slab = x[:, s * 128:(s + 1) * 128]
            o_ref[:, s * 128:(s + 1) * 128] = (_rope_slab(slab, cos_p, sin_p) * scale).astype(o_ref.dtype)


def q_up(a, w_q, cos_p, sin_p, scale, tm=2048, tn=1024):
    m, kdim = a.shape
    tm = min(tm, m)
    n = w_q.shape[1]
    return pl.pallas_call(
        functools.partial(_q_up_kernel, scale=scale, n_nope_tiles=WIDTH // tn),
        grid=(m // tm, n // tn),
        in_specs=[pl.BlockSpec((tm, kdim), lambda i, j: (i, 0)),
                  pl.BlockSpec((kdim, tn), lambda i, j: (0, j)),
                  pl.BlockSpec((tm, 128), lambda i, j: (i, 0)),
                  pl.BlockSpec((tm, 128), lambda i, j: (i, 0))],
        out_specs=pl.BlockSpec((tm, tn), lambda i, j: (i, j)),
        out_shape=jax.ShapeDtypeStruct((m, n), BF16),
        compiler_params=_params(2),
        name="q_up",
    )(a, w_q, cos_p, sin_p)


def _qk(q, k):
    return lax.dot_general(q, k, (((1,), (1,)), ((), ())), preferred_element_type=F32)


def _softmax_update(s, v, m_ref, l_ref, acc_ref):
    m_prev = m_ref[...]
    m_new = jnp.maximum(m_prev, jnp.max(s, axis=-1, keepdims=True))
    alpha = jnp.exp2(m_prev - m_new)
    p = jnp.exp2(s - jnp.concatenate([m_new] * (s.shape[1] // 128), axis=1))
    l_ref[...] = alpha * l_ref[...] + jnp.sum(p, axis=-1, keepdims=True)
    acc_ref[...] = alpha * acc_ref[...] + jnp.dot(p.astype(BF16), v, preferred_element_type=F32)
    m_ref[...] = m_new


def _softmax_reset(m_ref, l_ref, acc_ref):
    m_ref[...] = jnp.full(m_ref.shape, MASK, F32)
    l_ref[...] = jnp.zeros(l_ref.shape, F32)
    acc_ref[...] = jnp.zeros(acc_ref.shape, F32)


def _softmax_scratch(nq):
    return [pltpu.VMEM((nq, ATT_TQ, 128), F32), pltpu.VMEM((nq, ATT_TQ, 128), F32),
            pltpu.VMEM((nq, ATT_TQ, HEAD_DIM), F32)]


def _head_block(seq, col):
    return pl.BlockSpec((seq, HEAD_DIM), col)


def _emit_head(rows, g_ref, og_ref, ssq_ref, l_ref, acc_ref):
    o = acc_ref[...] / l_ref[...]
    og_ref[rows, :] = (o * g_ref[...]).astype(og_ref.dtype)
    ssq_ref[rows, :] += jnp.sum(o * o, axis=-1, keepdims=True)


def _mla_kernel(qn_ref, qr_ref, kn_ref, kr_ref, v_ref, g_ref, og_ref, ssq_ref,
                kc_ref, m_ref, l_ref, acc_ref):
    @pl.when(pl.program_id(1) == 0)
    def _():
        ssq_ref[...] = jnp.zeros_like(ssq_ref)

    kc_ref[:, :HEAD_DIM] = kn_ref[...]
    kc_ref[:, HEAD_DIM:] = kr_ref[...]
    seq = og_ref.shape[0]
    row = lax.broadcasted_iota(jnp.int32, (ATT_TQ, ATT_TK), 0)
    col = lax.broadcasted_iota(jnp.int32, (ATT_TQ, ATT_TK), 1)
    for qi in range(seq // ATT_TQ):
        rows = pl.ds(qi * ATT_TQ, ATT_TQ)
        q = jnp.concatenate([qn_ref[rows, :], qr_ref[rows, :]], axis=-1)
        state = (m_ref.at[qi], l_ref.at[qi], acc_ref.at[qi])
        _softmax_reset(*state)
        for j in range(qi + 1):
            keys = pl.ds(j * ATT_TK, ATT_TK)
            s = _qk(q, kc_ref[keys, :])
            if j == qi:
                s = jnp.where(row >= col, s, MASK)
            _softmax_update(s, v_ref[keys, :], *state)
        _emit_head(rows, g_ref, og_ref, ssq_ref, l_ref.at[qi], acc_ref.at[qi])


def _group_out(batch, seq):
    specs = [_head_block(seq, lambda b, h: (b, h)), pl.BlockSpec((seq, 1), lambda b, h: (b, 0))]
    shapes = [jax.ShapeDtypeStruct((batch * seq, WIDTH), BF16),
              jax.ShapeDtypeStruct((batch * seq, 1), F32)]
    return specs, shapes


def mla_attention(q_all, kv, k_r, g_out, batch, seq):
    out_specs, out_shape = _group_out(batch, seq)
    return pl.pallas_call(
        _mla_kernel,
        grid=(batch, N_HEADS),
        in_specs=[_head_block(seq, lambda b, h: (b, h)),
                  _head_block(seq, lambda b, h: (b, N_HEADS + h)),
                  _head_block(seq, lambda b, h: (b, 2 * h)),
                  _head_block(seq, lambda b, h: (b, 0)),
                  _head_block(seq, lambda b, h: (b, 2 * h + 1)),
                  pl.BlockSpec((1, HEAD_DIM), lambda b, h: (0, h))],
        out_specs=out_specs,
        out_shape=out_shape,
        scratch_shapes=[pltpu.VMEM((seq, 2 * HEAD_DIM), BF16)] + _softmax_scratch(seq // ATT_TQ),
        compiler_params=_params(2),
        name="mla_attention",
    )(q_all, q_all, kv, k_r, kv, g_out.reshape(1, WIDTH))


def _dilated_tables(off):
    row = lax.broadcasted_iota(jnp.int32, (ATT_TQ, ATT_TK), 0)
    col = lax.broadcasted_iota(jnp.int32, (ATT_TQ, ATT_TK), 1)
    dist = row - col + off * ATT_TK
    mult = jnp.zeros((ATT_TQ, ATT_TK), jnp.int32)
    for window, dilation in DIL_CONFIGS:
        assert dilation & (dilation - 1) == 0
        held = (dist >= 0) & (dist <= window) & ((dist & (dilation - 1)) == 0)
        mult = mult + held.astype(jnp.int32)
    log2_mult = jnp.full((ATT_TQ, ATT_TK), MASK, F32)
    for n in range(1, len(DIL_CONFIGS) + 1):
        log2_mult = jnp.where(mult == n, jnp.float32(math.log2(n)), log2_mult)
    return dist.astype(F32), log2_mult


def _dilated_kernel(slope_ref, q_ref, k_ref, v_ref, g_ref, og_ref, ssq_ref, dist_ref, lmult_ref, bias_ref,
                    m_ref, l_ref, acc_ref):
    @pl.when(jnp.logical_and(pl.program_id(0) == 0, pl.program_id(1) == 0))
    def _():
        for off in range(DIL_NOFF):
            dist_ref[off], lmult_ref[off] = _dilated_tables(off)

    @pl.when(pl.program_id(1) == 0)
    def _():
        ssq_ref[...] = jnp.zeros_like(ssq_ref)

    neg_slope = -slope_ref[pl.program_id(1)] * LOG2E
    for off in range(DIL_NOFF):
        bias_ref[off] = neg_slope * dist_ref[off] + lmult_ref[off]
    seq = og_ref.shape[0]
    for qi in range(seq // ATT_TQ):
        rows = pl.ds(qi * ATT_TQ, ATT_TQ)
        q = q_ref[rows, :]
        state = (m_ref.at[qi], l_ref.at[qi], acc_ref.at[qi])
        _softmax_reset(*state)
        for off in range(min(qi, DIL_NOFF - 1), -1, -1):
            keys = pl.ds((qi - off) * ATT_TK, ATT_TK)
            s = _qk(q, k_ref[keys, :]) + bias_ref[off]
            _softmax_update(s, v_ref[keys, :], *state)
        _emit_head(rows, g_ref, og_ref, ssq_ref, l_ref.at[qi], acc_ref.at[qi])


def dilated_attention(qkv, slopes, g_out, batch, seq):
    out_specs, out_shape = _group_out(batch, seq)
    return pl.pallas_call(
        _dilated_kernel,
        grid=(batch, N_HEADS),
        in_specs=[pl.BlockSpec(memory_space=pltpu.SMEM),
                  _head_block(seq, lambda b, h: (b, h)),
                  _head_block(seq, lambda b, h: (b, N_HEADS + h)),
                  _head_block(seq, lambda b, h: (b, 2 * N_HEADS + h)),
                  pl.BlockSpec((1, HEAD_DIM), lambda b, h: (0, h))],
        out_specs=out_specs,
        out_shape=out_shape,
        scratch_shapes=[pltpu.VMEM((DIL_NOFF, ATT_TQ, ATT_TK), F32)] * 3 + _softmax_scratch(seq // ATT_TQ),
        compiler_params=_params(2),
        name="dilated_attention",
    )(slopes, qkv, qkv, qkv, g_out.reshape(1, WIDTH))


def kernel(x, positions, ffn1_norm, ffn1_w_gate, ffn1_w_up, ffn1_w_down, mix_norm, w_in, g_cq, w_uq,
           g_ckv, w_ukv, g_out_a, g_out_b, w_o, ffn2_norm, ffn2_w_gate, ffn2_w_up, ffn2_w_down,
           final_norm):
    batch, seq, d = x.shape
    m = batch * seq
    depth = w_in.shape[0]

    inv_freq = 1.0 / (ROPE_THETA ** (jnp.arange(0, ROPE_DIM, 2, dtype=F32) / ROPE_DIM))
    ang = positions.astype(F32)[..., None] * inv_freq
    cos = jnp.cos(ang).reshape(m, ROPE_DIM // 2)
    sin = jnp.sin(ang).reshape(m, ROPE_DIM // 2)
    zeros = jnp.zeros((m, 128 - ROPE_DIM), F32)
    cos_p = jnp.concatenate([cos, cos, zeros], axis=-1)
    sin_p = jnp.concatenate([-sin, sin, zeros], axis=-1)

    slopes = jnp.asarray(
        np.array([2.0 ** (-8.0 * (i + 1) / N_HEADS) for i in range(N_HEADS)], dtype=np.float32))
    scale_a = HEAD_DIM ** -0.5 * LOG2E
    scale_b = (HEAD_DIM + ROPE_DIM) ** -0.5 * LOG2E

    w_in_t = jnp.swapaxes(w_in, 1, 2)

    h = x.reshape(m, d)
    for l in range(depth):
        n = rmsnorm(h, ffn1_norm[l], BF16)
        hid, w_down = ffn_up(n, ffn1_w_gate, ffn1_w_up, ffn1_w_down, l)
        h, hg, ssq = matmul_ksplit_res(hid, w_down, h, 0.5, g_next=mix_norm[l])

        qkv_a = matmul_fullk_t(hg, ssq, w_in_t, l, out_dtype=BF16, n_cols=3 * WIDTH, scale=scale_a,
                               n_scaled=WIDTH // 512)
        w_c_t = jnp.concatenate(
            [w_in_t[l, 3 * WIDTH:, :], jnp.zeros((128 - ROPE_DIM, d), F32)], axis=0)
        cq_n, ckv_n, k_r = proj_c(hg, ssq, w_c_t, g_cq[l], g_ckv[l], cos_p, sin_p)

        out_a, ssq_a = dilated_attention(qkv_a, slopes, g_out_a[l], batch, seq)

        wq = w_uq[l].reshape(Q_LORA, N_HEADS, HEAD_DIM + ROPE_DIM)
        wq_rope = jnp.pad(wq[:, :, HEAD_DIM:], ((0, 0), (0, 0), (0, 128 - ROPE_DIM)))
        w_q = jnp.concatenate([wq[:, :, :HEAD_DIM].reshape(Q_LORA, WIDTH),
                               wq_rope.reshape(Q_LORA, N_HEADS * 128)], axis=-1)
        q_all = q_up(cq_n, w_q, cos_p, sin_p, scale_b)
        kv = matmul_fullk(ckv_n, w_ukv, l, out_dtype=BF16, tn=1024)
        out_b, ssq_b = mla_attention(q_all, kv, k_r, g_out_b[l], batch, seq)

        h = out_proj(out_a, out_b, ssq_a, ssq_b, w_o, l, h)

        n = rmsnorm(h, ffn2_norm[l], BF16)
        hid, w_down = ffn_up(n, ffn2_w_gate, ffn2_w_up, ffn2_w_down, l)
        h = matmul_ksplit_res(hid, w_down, h, 0.5)

    return rmsnorm(h, final_norm, F32).reshape(batch, seq, d)
```

```python
import functools
import math

import numpy as np
import jax
import jax.numpy as jnp
from jax import lax
from jax.experimental import pallas as pl
from jax.experimental.pallas import tpu as pltpu

F32 = jnp.float32
BF16 = jnp.bfloat16

HEAD_DIM = 128
N_HEADS = 16
WIDTH = N_HEADS * HEAD_DIM
DIL_CONFIGS = ((128, 1), (512, 4), (2048, 16))
ROPE_DIM = 64
Q_LORA = 1024
KV_LORA = 512
ROPE_THETA = 10000.0
EPS = 1e-6
MASK = -1e30
LOG2E = math.log2(math.e)

VMEM_LIMIT = 56 * 1024 * 1024
ATT_TQ = 512
ATT_TK = 512
DIL_MAX_DIST = max(w for w, _ in DIL_CONFIGS)
DIL_NOFF = DIL_MAX_DIST // ATT_TK + 1


def _params(n_axes):
    return pltpu.CompilerParams(
        dimension_semantics=("arbitrary",) * n_axes, vmem_limit_bytes=VMEM_LIMIT)


def _rmsnorm_kernel(x_ref, g_ref, o_ref):
    x = x_ref[...]
    ms = jnp.mean(x * x, axis=-1, keepdims=True)
    o_ref[...] = ((x * lax.rsqrt(ms + EPS)) * g_ref[...]).astype(o_ref.dtype)


def rmsnorm(x, g, out_dtype, tm=512):
    m, d = x.shape
    return pl.pallas_call(
        _rmsnorm_kernel,
        grid=(m // tm,),
        in_specs=[pl.BlockSpec((tm, d), lambda i: (i, 0)),
                  pl.BlockSpec((1, d), lambda i: (0, 0))],
        out_specs=pl.BlockSpec((tm, d), lambda i: (i, 0)),
        out_shape=jax.ShapeDtypeStruct((m, d), out_dtype),
        compiler_params=_params(1),
        name="rmsnorm",
    )(x, g.reshape(1, d))


def _rstd(ssq_ref, dim):
    return lax.rsqrt(ssq_ref[...] * (1.0 / dim) + EPS)


def _dot(a, w):
    return jnp.dot(a, w.astype(BF16), preferred_element_type=F32)


def _dot_t(a, w_t):
    return lax.dot_general(a, w_t.astype(BF16), (((1,), (1,)), ((), ())), preferred_element_type=F32)


def _w_spec(w, layer, block, index_map):
    if w.ndim == 3:
        return pl.BlockSpec((None,) + block, lambda *g: (layer,) + index_map(*g))
    return pl.BlockSpec(block, index_map)


def _resident_spec(tm, kdim):
    return pl.BlockSpec((tm, kdim), lambda i, j: (i, 0), pipeline_mode=pl.Buffered(1))


def _mm_fullk_kernel(a_ref, w_ref, o_ref, *, scale, n_scaled):
    s = jnp.where(pl.program_id(1) < n_scaled, jnp.float32(scale), jnp.float32(1.0))
    o_ref[...] = (_dot(a_ref[...], w_ref[...]) * s).astype(o_ref.dtype)


def _out_proj_kernel(a_ref, b_ref, ssq_a_ref, ssq_b_ref, w_ref, r_ref, o_ref):
    ka = a_ref.shape[1]
    ya = _dot(a_ref[...], w_ref[:ka, :]) * _rstd(ssq_a_ref, ka)
    yb = _dot(b_ref[...], w_ref[ka:, :]) * _rstd(ssq_b_ref, b_ref.shape[1])
    o_ref[...] = r_ref[...] + ya + yb


def out_proj(a, b, ssq_a, ssq_b, w, layer, residual, tm=1024, tn=512):
    m, ka = a.shape
    kb = b.shape[1]
    n = w.shape[-1]
    tn = min(tn, n)
    tm = min(tm, m)
    rows = lambda width: pl.BlockSpec((tm, width), lambda i, j: (i, 0))
    o_spec = pl.BlockSpec((tm, tn), lambda i, j: (i, j))
    return pl.pallas_call(
        _out_proj_kernel,
        grid=(m // tm, n // tn),
        in_specs=[rows(ka), rows(kb), rows(1), rows(1),
                  _w_spec(w, layer, (ka + kb, tn), lambda i, j: (0, j)), o_spec],
        out_specs=o_spec,
        out_shape=jax.ShapeDtypeStruct((m, n), F32),
        compiler_params=_params(2),
        name="out_proj",
    )(a, b, ssq_a, ssq_b, w, residual)


def _mm_fullk_t_kernel(a_ref, ssq_ref, w_ref, o_ref, *, scale, n_scaled):
    s = jnp.where(pl.program_id(1) < n_scaled, jnp.float32(scale), jnp.float32(1.0))
    r = _rstd(ssq_ref, a_ref.shape[1]) * s
    o_ref[...] = (_dot_t(a_ref[...], w_ref[...]) * r).astype(o_ref.dtype)


def matmul_fullk_t(a, ssq, w_t, layer, *, out_dtype, n_cols, scale=1.0, n_scaled=0, tm=2048, tn=512):
    m, kdim = a.shape
    tm = min(tm, m)
    return pl.pallas_call(
        functools.partial(_mm_fullk_t_kernel, scale=scale, n_scaled=n_scaled),
        grid=(m // tm, n_cols // tn),
        in_specs=[_resident_spec(tm, kdim),
                  pl.BlockSpec((tm, 1), lambda i, j: (i, 0)),
                  pl.BlockSpec((None, tn, kdim), lambda i, j: (layer, j, 0))],
        out_specs=pl.BlockSpec((tm, tn), lambda i, j: (i, j)),
        out_shape=jax.ShapeDtypeStruct((m, n_cols), out_dtype),
        compiler_params=_params(2),
        name="matmul_fullk_t",
    )(a, ssq, w_t)


def matmul_fullk(a, w, layer, *, out_dtype, tm=2048, tn=256):
    m, kdim = a.shape
    n = w.shape[-1]
    tn = min(tn, n)
    tm = min(tm, m)
    return pl.pallas_call(
        functools.partial(_mm_fullk_kernel, scale=1.0, n_scaled=0),
        grid=(m // tm, n // tn),
        in_specs=[_resident_spec(tm, kdim), _w_spec(w, layer, (kdim, tn), lambda i, j: (0, j))],
        out_specs=pl.BlockSpec((tm, tn), lambda i, j: (i, j)),
        out_shape=jax.ShapeDtypeStruct((m, n), out_dtype),
        compiler_params=_params(2),
        name="matmul_fullk",
    )(a, w)


def _ffn_up_kernel(a_ref, wg_ref, wu_ref, wd_ref, o_ref, wd_bf16_ref, *, n_chunks):
    step = pl.program_id(0) * pl.num_programs(1) + pl.program_id(1)

    @pl.when(step < n_chunks)
    def _():
        wd_bf16_ref[...] = wd_ref[...].astype(BF16)

    a = a_ref[...]
    g = _dot(a, wg_ref[...])
    u = _dot(a, wu_ref[...])
    o_ref[...] = (g * jax.nn.sigmoid(g) * u).astype(o_ref.dtype)


def ffn_up(a, w_gate, w_up, w_down, layer, tm=2048, tn=256):
    m, kdim = a.shape
    n = w_gate.shape[-1]
    tm = min(tm, m)
    n_j = n // tn
    n_steps = (m // tm) * n_j
    chunk = next(c for c in range(128, n + 1, 128) if n % c == 0 and n // c <= n_steps)
    n_chunks = n // chunk
    d_out = w_down.shape[-1]
    chunk_of = lambda i, j: jnp.minimum(i * n_j + j, n_chunks - 1)
    w_spec = _w_spec(w_gate, layer, (kdim, tn), lambda i, j: (0, j))
    return pl.pallas_call(
        functools.partial(_ffn_up_kernel, n_chunks=n_chunks),
        grid=(m // tm, n_j),
        in_specs=[_resident_spec(tm, kdim), w_spec, w_spec,
                  pl.BlockSpec((None, chunk, d_out), lambda i, j: (layer, chunk_of(i, j), 0))],
        out_specs=[pl.BlockSpec((tm, tn), lambda i, j: (i, j)),
                   pl.BlockSpec((chunk, d_out), lambda i, j: (chunk_of(i, j), 0))],
        out_shape=[jax.ShapeDtypeStruct((m, n), BF16),
                   jax.ShapeDtypeStruct((n, d_out), BF16)],
        compiler_params=_params(2),
        name="ffn_up",
    )(a, w_gate, w_up, w_down)


def _mm_ksplit_res_kernel(a_ref, w_ref, r_ref, *rest, nk, k_rem, scale, emit):
    if emit:
        g_ref, o_ref, og_ref, ssq_ref, acc_ref = rest
    else:
        o_ref, acc_ref = rest
    k = pl.program_id(2)

    @pl.when(k == 0)
    def _():
        acc_ref[...] = _dot(a_ref[...], w_ref[...])

    @pl.when(jnp.logical_and(k > 0, k < nk - 1))
    def _():
        acc_ref[...] += _dot(a_ref[...], w_ref[...])

    @pl.when(k == nk - 1)
    def _():
        tail = _dot(a_ref[:, :k_rem], w_ref[:k_rem, :])
        o = r_ref[...] + scale * (acc_ref[...] + tail)
        o_ref[...] = o
        if emit:
            og_ref[...] = (o * g_ref[...]).astype(og_ref.dtype)
            part = jnp.sum(o * o, axis=-1, keepdims=True)
            j = pl.program_id(1)

            @pl.when(j == 0)
            def _():
                ssq_ref[...] = part

            @pl.when(j > 0)
            def _():
                ssq_ref[...] += part


def matmul_ksplit_res(a, w, residual, scale, g_next=None, tm=1024, tn=1024, tk=2816):
    m, kdim = a.shape
    n = w.shape[-1]
    tn = min(tn, n)
    tk = min(tk, kdim)
    nk = pl.cdiv(kdim, tk)
    assert nk >= 2
    k_rem = kdim - (nk - 1) * tk
    emit = g_next is not None
    o_spec = pl.BlockSpec((tm, tn), lambda i, j, k: (i, j))
    in_specs = [pl.BlockSpec((tm, tk), lambda i, j, k: (i, k)),
                pl.BlockSpec((tk, tn), lambda i, j, k: (k, j)),
                o_spec]
    args = (a, w, residual)
    out_specs, out_shape = o_spec, jax.ShapeDtypeStruct((m, n), F32)
    if emit:
        in_specs.append(pl.BlockSpec((1, tn), lambda i, j, k: (0, j)))
        args += (g_next.reshape(1, n),)
        out_specs = [o_spec, o_spec, pl.BlockSpec((tm, 1), lambda i, j, k: (i, 0))]
        out_shape = [out_shape, jax.ShapeDtypeStruct((m, n), BF16), jax.ShapeDtypeStruct((m, 1), F32)]
    return pl.pallas_call(
        functools.partial(_mm_ksplit_res_kernel, nk=nk, k_rem=k_rem, scale=scale, emit=emit),
        grid=(m // tm, n // tn, nk),
        in_specs=in_specs,
        out_specs=out_specs,
        out_shape=out_shape,
        scratch_shapes=[pltpu.VMEM((tm, tn), F32)],
        compiler_params=_params(3),
        name="matmul_ksplit_res",
    )(*args)


def _rope_slab(x, cos_p, sin_p):
    lane = lax.broadcasted_iota(jnp.int32, x.shape, 1)
    half = ROPE_DIM // 2
    swapped = jnp.where(lane < half, pltpu.roll(x, 128 - half, 1), pltpu.roll(x, half, 1))
    return x * cos_p + swapped * sin_p


def _proj_c_kernel(a_ref, ssq_ref, w_ref, gq_ref, gkv_ref, cos_ref, sin_ref,
                   cq_ref, ckv_ref, kr_ref, acc_ref, *, nk, dim):
    k = pl.program_id(1)

    @pl.when(k == 0)
    def _():
        acc_ref[...] = _dot_t(a_ref[...], w_ref[...])

    @pl.when(k > 0)
    def _():
        acc_ref[...] += _dot_t(a_ref[...], w_ref[...])

    @pl.when(k == nk - 1)
    def _():
        def norm(x, g):
            ms = jnp.mean(x * x, axis=-1, keepdims=True)
            return (x * lax.rsqrt(ms + EPS)) * g
        r = _rstd(ssq_ref, dim)
        cq_ref[...] = norm(acc_ref[:, :Q_LORA] * r, gq_ref[...]).astype(cq_ref.dtype)
        ckv_ref[...] = norm(acc_ref[:, Q_LORA:Q_LORA + KV_LORA] * r, gkv_ref[...]).astype(ckv_ref.dtype)
        kr = acc_ref[:, Q_LORA + KV_LORA:] * r
        kr_ref[...] = _rope_slab(kr, cos_ref[...], sin_ref[...]).astype(kr_ref.dtype)


def proj_c(a, ssq, w_c_t, g_cq, g_ckv, cos_p, sin_p, tm=1024, tk=1024):
    m, kdim = a.shape
    n = w_c_t.shape[0]
    tm = min(tm, m)
    tk = min(tk, kdim)
    nk = kdim // tk
    row = lambda width: pl.BlockSpec((tm, width), lambda i, k: (i, 0))
    vec = lambda width: pl.BlockSpec((1, width), lambda i, k: (0, 0))
    return pl.pallas_call(
        functools.partial(_proj_c_kernel, nk=nk, dim=kdim),
        grid=(m // tm, nk),
        in_specs=[pl.BlockSpec((tm, tk), lambda i, k: (i, k)),
                  row(1),
                  pl.BlockSpec((n, tk), lambda i, k: (0, k)),
                  vec(Q_LORA), vec(KV_LORA), row(128), row(128)],
        out_specs=[row(Q_LORA), row(KV_LORA), row(128)],
        out_shape=[jax.ShapeDtypeStruct((m, Q_LORA), BF16),
                   jax.ShapeDtypeStruct((m, KV_LORA), BF16),
                   jax.ShapeDtypeStruct((m, 128), BF16)],
        scratch_shapes=[pltpu.VMEM((tm, n), F32)],
        compiler_params=_params(2),
        name="proj_c",
    )(a, ssq, w_c_t, g_cq.reshape(1, Q_LORA), g_ckv.reshape(1, KV_LORA), cos_p, sin_p)


def _q_up_kernel(a_ref, w_ref, cos_ref, sin_ref, qn_ref, qr_ref, *, scale):
    x = _dot(a_ref[...], w_ref[...])
    cos_p = cos_ref[...]
    sin_p = sin_ref[...]
    low = lax.broadcasted_iota(jnp.int32, (x.shape[0], 128), 1) < ROPE_DIM
    for pair in range(x.shape[1] // 384):
        v0, v1, v2 = (x[:, (3 * pair + t) * 128:(3 * pair + t + 1) * 128] for t in range(3))
        v1r = pltpu.roll(v1, 64, 1)
        v2r = pltpu.roll(v2, 64, 1)
        heads = ((v0, jnp.where(low, v1, 0.0)),
                 (jnp.where(low, v1r, v2r), jnp.where(low, v2r, 0.0)))
        for t, (nope, rope) in enumerate(heads):
            cols = pl.ds((2 * pair + t) * 128, 128)
            qn_ref[:, cols] = (nope * scale).astype(qn_ref.dtype)
            qr_ref[:, cols] = (_rope_slab(rope, cos_p, sin_p) * scale).astype(qr_ref.dtype)


def q_up(a, w_uq, layer, cos_p, sin_p, scale, tm=2048, heads_per_tile=4):
    m, kdim = a.shape
    tm = min(tm, m)
    tn_in = heads_per_tile * (HEAD_DIM + ROPE_DIM)
    tn_out = heads_per_tile * 128
    o_spec = pl.BlockSpec((tm, tn_out), lambda i, j: (i, j))
    return pl.pallas_call(
        functools.partial(_q_up_kernel, scale=scale),
        grid=(m // tm, N_HEADS // heads_per_tile),
        in_specs=[pl.BlockSpec((tm, kdim), lambda i, j: (i, 0)),
                  pl.BlockSpec((None, kdim, tn_in), lambda i, j: (layer, 0, j)),
                  pl.BlockSpec((tm, 128), lambda i, j: (i, 0)),
                  pl.BlockSpec((tm, 128), lambda i, j: (i, 0))],
        out_specs=[o_spec, o_spec],
        out_shape=[jax.ShapeDtypeStruct((m, WIDTH), BF16), jax.ShapeDtypeStruct((m, WIDTH), BF16)],
        compiler_params=_params(2),
        name="q_up",
    )(a, w_uq, cos_p, sin_p)


def _qk(q, k):
    return lax.dot_general(q, k, (((1,), (1,)), ((), ())), preferred_element_type=F32)


def _softmax_update(s, v, m_ref, l_ref, acc_ref):
    m_prev = m_ref[...]
    m_new = jnp.maximum(m_prev, jnp.max(s, axis=-1, keepdims=True))
    alpha = jnp.exp2(m_prev - m_new)
    p = jnp.exp2(s - jnp.concatenate([m_new] * (s.shape[1] // 128), axis=1))
    l_ref[...] = alpha * l_ref[...] + jnp.sum(p, axis=-1, keepdims=True)
    acc_ref[...] = alpha * acc_ref[...] + jnp.dot(p.astype(BF16), v, preferred_element_type=F32)
    m_ref[...] = m_new


def _softmax_reset(m_ref, l_ref, acc_ref):
    m_ref[...] = jnp.full(m_ref.shape, MASK, F32)
    l_ref[...] = jnp.zeros(l_ref.shape, F32)
    acc_ref[...] = jnp.zeros(acc_ref.shape, F32)


def _softmax_scratch(nq):
    return [pltpu.VMEM((nq, ATT_TQ, 128), F32), pltpu.VMEM((nq, ATT_TQ, 128), F32),
            pltpu.VMEM((nq, ATT_TQ, HEAD_DIM), F32)]


def _head_block(seq, col):
    return pl.BlockSpec((seq, HEAD_DIM), col)


def _emit_head(rows, g_ref, og_ref, ssq_ref, l_ref, acc_ref):
    o = acc_ref[...] / l_ref[...]
    og_ref[rows, :] = (o * g_ref[...]).astype(og_ref.dtype)
    ssq_ref[rows, :] += jnp.sum(o * o, axis=-1, keepdims=True)


def _mla_kernel(qn_ref, qr_ref, kn_ref, kr_ref, v_ref, g_ref, og_ref, ssq_ref,
                kc_ref, m_ref, l_ref, acc_ref):
    @pl.when(pl.program_id(1) == 0)
    def _():
        ssq_ref[...] = jnp.zeros_like(ssq_ref)

    kc_ref[:, :HEAD_DIM] = kn_ref[...]
    kc_ref[:, HEAD_DIM:] = kr_ref[...]
    seq = og_ref.shape[0]
    row = lax.broadcasted_iota(jnp.int32, (ATT_TQ, ATT_TK), 0)
    col = lax.broadcasted_iota(jnp.int32, (ATT_TQ, ATT_TK), 1)
    for qi in range(seq // ATT_TQ):
        rows = pl.ds(qi * ATT_TQ, ATT_TQ)
        q = jnp.concatenate([qn_ref[rows, :], qr_ref[rows, :]], axis=-1)
        state = (m_ref.at[qi], l_ref.at[qi], acc_ref.at[qi])
        _softmax_reset(*state)
        for j in range(qi + 1):
            keys = pl.ds(j * ATT_TK, ATT_TK)
            s = _qk(q, kc_ref[keys, :])
            if j == qi:
                s = jnp.where(row >= col, s, MASK)
            _softmax_update(s, v_ref[keys, :], *state)
        _emit_head(rows, g_ref, og_ref, ssq_ref, l_ref.at[qi], acc_ref.at[qi])


def _group_out(batch, seq):
    specs = [_head_block(seq, lambda b, h: (b, h)), pl.BlockSpec((seq, 1), lambda b, h: (b, 0))]
    shapes = [jax.ShapeDtypeStruct((batch * seq, WIDTH), BF16),
              jax.ShapeDtypeStruct((batch * seq, 1), F32)]
    return specs, shapes


def mla_attention(q_nope, q_rope, kv, k_r, g_out, batch, seq):
    out_specs, out_shape = _group_out(batch, seq)
    return pl.pallas_call(
        _mla_kernel,
        grid=(batch, N_HEADS),
        in_specs=[_head_block(seq, lambda b, h: (b, h)),
                  _head_block(seq, lambda b, h: (b, h)),
                  _head_block(seq, lambda b, h: (b, 2 * h)),
                  _head_block(seq, lambda b, h: (b, 0)),
                  _head_block(seq, lambda b, h: (b, 2 * h + 1)),
                  pl.BlockSpec((1, HEAD_DIM), lambda b, h: (0, h))],
        out_specs=out_specs,
        out_shape=out_shape,
        scratch_shapes=[pltpu.VMEM((seq, 2 * HEAD_DIM), BF16)] + _softmax_scratch(seq // ATT_TQ),
        compiler_params=_params(2),
        name="mla_attention",
    )(q_nope, q_rope, kv, k_r, kv, g_out.reshape(1, WIDTH))


def _dilated_tables(off):
    row = lax.broadcasted_iota(jnp.int32, (ATT_TQ, ATT_TK), 0)
    col = lax.broadcasted_iota(jnp.int32, (ATT_TQ, ATT_TK), 1)
    dist = row - col + off * ATT_TK
    mult = jnp.zeros((ATT_TQ, ATT_TK), jnp.int32)
    for window, dilation in DIL_CONFIGS:
        assert dilation & (dilation - 1) == 0
        held = (dist >= 0) & (dist <= window) & ((dist & (dilation - 1)) == 0)
        mult = mult + held.astype(jnp.int32)
    log2_mult = jnp.full((ATT_TQ, ATT_TK), MASK, F32)
    for n in range(1, len(DIL_CONFIGS) + 1):
        log2_mult = jnp.where(mult == n, jnp.float32(math.log2(n)), log2_mult)
    return dist.astype(F32), log2_mult


def _dilated_kernel(slope_ref, q_ref, k_ref, v_ref, g_ref, og_ref, ssq_ref, dist_ref, lmult_ref, bias_ref,
                    m_ref, l_ref, acc_ref):
    @pl.when(jnp.logical_and(pl.program_id(0) == 0, pl.program_id(1) == 0))
    def _():
        for off in range(DIL_NOFF):
            dist_ref[off], lmult_ref[off] = _dilated_tables(off)

    @pl.when(pl.program_id(1) == 0)
    def _():
        ssq_ref[...] = jnp.zeros_like(ssq_ref)

    neg_slope = -slope_ref[pl.program_id(1)] * LOG2E
    for off in range(DIL_NOFF):
        bias_ref[off] = neg_slope * dist_ref[off] + lmult_ref[off]
    seq = og_ref.shape[0]
    for qi in range(seq // ATT_TQ):
        rows = pl.ds(qi * ATT_TQ, ATT_TQ)
        q = q_ref[rows, :]
        state = (m_ref.at[qi], l_ref.at[qi], acc_ref.at[qi])
        _softmax_reset(*state)
        for off in range(min(qi, DIL_NOFF - 1), -1, -1):
            keys = pl.ds((qi - off) * ATT_TK, ATT_TK)
            s = _qk(q, k_ref[keys, :]) + bias_ref[off]
            _softmax_update(s, v_ref[keys, :], *state)
        _emit_head(rows, g_ref, og_ref, ssq_ref, l_ref.at[qi], acc_ref.at[qi])


def dilated_attention(qkv, slopes, g_out, batch, seq):
    out_specs, out_shape = _group_out(batch, seq)
    return pl.pallas_call(
        _dilated_kernel,
        grid=(batch, N_HEADS),
        in_specs=[pl.BlockSpec(memory_space=pltpu.SMEM),
                  _head_block(seq, lambda b, h: (b, h)),
                  _head_block(seq, lambda b, h: (b, N_HEADS + h)),
                  _head_block(seq, lambda b, h: (b, 2 * N_HEADS + h)),
                  pl.BlockSpec((1, HEAD_DIM), lambda b, h: (0, h))],
        out_specs=out_specs,
        out_shape=out_shape,
        scratch_shapes=[pltpu.VMEM((DIL_NOFF, ATT_TQ, ATT_TK), F32)] * 3 + _softmax_scratch(seq // ATT_TQ),
        compiler_params=_params(2),
        name="dilated_attention",
    )(slopes, qkv, qkv, qkv, g_out.reshape(1, WIDTH))


def kernel(x, positions, ffn1_norm, ffn1_w_gate, ffn1_w_up, ffn1_w_down, mix_norm, w_in, g_cq, w_uq,
           g_ckv, w_ukv, g_out_a, g_out_b, w_o, ffn2_norm, ffn2_w_gate, ffn2_w_up, ffn2_w_down,
           final_norm):
    batch, seq, d = x.shape
    m = batch * seq
    depth = w_in.shape[0]

    inv_freq = 1.0 / (ROPE_THETA ** (jnp.arange(0, ROPE_DIM, 2, dtype=F32) / ROPE_DIM))
    ang = positions.astype(F32)[..., None] * inv_freq
    cos = jnp.cos(ang).reshape(m, ROPE_DIM // 2)
    sin = jnp.sin(ang).reshape(m, ROPE_DIM // 2)
    zeros = jnp.zeros((m, 128 - ROPE_DIM), F32)
    cos_p = jnp.concatenate([cos, cos, zeros], axis=-1)
    sin_p = jnp.concatenate([-sin, sin, zeros], axis=-1)

    slopes = jnp.asarray(
        np.array([2.0 ** (-8.0 * (i + 1) / N_HEADS) for i in range(N_HEADS)], dtype=np.float32))
    scale_a = HEAD_DIM ** -0.5 * LOG2E
    scale_b = (HEAD_DIM + ROPE_DIM) ** -0.5 * LOG2E

    w_in_t = jnp.swapaxes(w_in, 1, 2)

    h = x.reshape(m, d)
    for l in range(depth):
        n = rmsnorm(h, ffn1_norm[l], BF16)
        hid, w_down = ffn_up(n, ffn1_w_gate, ffn1_w_up, ffn1_w_down, l)
        h, hg, ssq = matmul_ksplit_res(hid, w_down, h, 0.5, g_next=mix_norm[l])

        qkv_a = matmul_fullk_t(hg, ssq, w_in_t, l, out_dtype=BF16, n_cols=3 * WIDTH, scale=scale_a,
                               n_scaled=WIDTH // 512)
        w_c_t = jnp.concatenate(
            [w_in_t[l, 3 * WIDTH:, :], jnp.zeros((128 - ROPE_DIM, d), F32)], axis=0)
        cq_n, ckv_n, k_r = proj_c(hg, ssq, w_c_t, g_cq[l], g_ckv[l], cos_p, sin_p)

        out_a, ssq_a = dilated_attention(qkv_a, slopes, g_out_a[l], batch, seq)

        q_nope, q_rope = q_up(cq_n, w_uq, l, cos_p, sin_p, scale_b)
        kv = matmul_fullk(ckv_n, w_ukv, l, out_dtype=BF16, tn=1024)
        out_b, ssq_b = mla_attention(q_nope, q_rope, kv, k_r, g_out_b[l], batch, seq)

        h = out_proj(out_a, out_b, ssq_a, ssq_b, w_o, l, h)

        n = rmsnorm(h, ffn2_norm[l], BF16)
        hid, w_down = ffn_up(n, ffn2_w_gate, ffn2_w_up, ffn2_w_down, l)
        h = matmul_ksplit_res(hid, w_down, h, 0.5)

    return rmsnorm(h, final_norm, F32).reshape(batch, seq, d)
```

```python
import functools
import math

import numpy as np
import jax
import jax.numpy as jnp
from jax import lax
from jax.experimental import pallas as pl
from jax.experimental.pallas import tpu as pltpu

F32 = jnp.float32
BF16 = jnp.bfloat16

HEAD_DIM = 128
N_HEADS = 16
WIDTH = N_HEADS * HEAD_DIM
DIL_CONFIGS = ((128, 1), (512, 4), (2048, 16))
ROPE_DIM = 64
Q_LORA = 1024
KV_LORA = 512
ROPE_THETA = 10000.0
EPS = 1e-6
MASK = -1e30
LOG2E = math.log2(math.e)

VMEM_LIMIT = 56 * 1024 * 1024
ATT_TQ = 512
ATT_TK = 512
DIL_MAX_DIST = max(w for w, _ in DIL_CONFIGS)
DIL_NOFF = DIL_MAX_DIST // ATT_TK + 1


def _params(n_axes):
    return pltpu.CompilerParams(
        dimension_semantics=("arbitrary",) * n_axes, vmem_limit_bytes=VMEM_LIMIT)


def _rmsnorm_kernel(x_ref, g_ref, o_ref):
    x = x_ref[...]
    ms = jnp.mean(x * x, axis=-1, keepdims=True)
    o_ref[...] = ((x * lax.rsqrt(ms + EPS)) * g_ref[...]).astype(o_ref.dtype)


def rmsnorm(x, g, out_dtype, tm=512):
    m, d = x.shape
    return pl.pallas_call(
        _rmsnorm_kernel,
        grid=(m // tm,),
        in_specs=[pl.BlockSpec((tm, d), lambda i: (i, 0)),
                  pl.BlockSpec((1, d), lambda i: (0, 0))],
        out_specs=pl.BlockSpec((tm, d), lambda i: (i, 0)),
        out_shape=jax.ShapeDtypeStruct((m, d), out_dtype),
        compiler_params=_params(1),
        name="rmsnorm",
    )(x, g.reshape(1, d))


def _rstd(ssq_ref, dim):
    return lax.rsqrt(ssq_ref[...] * (1.0 / dim) + EPS)


def _dot(a, w):
    return jnp.dot(a, w.astype(BF16), preferred_element_type=F32)


def _dot_t(a, w_t):
    return lax.dot_general(a, w_t.astype(BF16), (((1,), (1,)), ((), ())), preferred_element_type=F32)


def _w_spec(w, layer, block, index_map):
    if w.ndim == 3:
        return pl.BlockSpec((None,) + block, lambda *g: (layer,) + index_map(*g))
    return pl.BlockSpec(block, index_map)


def _resident_spec(tm, kdim):
    return pl.BlockSpec((tm, kdim), lambda i, j: (i, 0), pipeline_mode=pl.Buffered(1))


def _mm_fullk_kernel(a_ref, w_ref, o_ref, *, scale, n_scaled):
    s = jnp.where(pl.program_id(1) < n_scaled, jnp.float32(scale), jnp.float32(1.0))
    o_ref[...] = (_dot(a_ref[...], w_ref[...]) * s).astype(o_ref.dtype)


def _out_proj_kernel(a_ref, b_ref, ssq_a_ref, ssq_b_ref, w_ref, r_ref, o_ref):
    ka = a_ref.shape[1]
    ya = _dot(a_ref[...], w_ref[:ka, :]) * _rstd(ssq_a_ref, ka)
    yb = _dot(b_ref[...], w_ref[ka:, :]) * _rstd(ssq_b_ref, b_ref.shape[1])
    o_ref[...] = r_ref[...] + ya + yb


def out_proj(a, b, ssq_a, ssq_b, w, layer, residual, tm=1024, tn=512):
    m, ka = a.shape
    kb = b.shape[1]
    n = w.shape[-1]
    tn = min(tn, n)
    tm = min(tm, m)
    rows = lambda width: pl.BlockSpec((tm, width), lambda i, j: (i, 0))
    o_spec = pl.BlockSpec((tm, tn), lambda i, j: (i, j))
    return pl.pallas_call(
        _out_proj_kernel,
        grid=(m // tm, n // tn),
        in_specs=[rows(ka), rows(kb), rows(1), rows(1),
                  _w_spec(w, layer, (ka + kb, tn), lambda i, j: (0, j)), o_spec],
        out_specs=o_spec,
        out_shape=jax.ShapeDtypeStruct((m, n), F32),
        compiler_params=_params(2),
        name="out_proj",
    )(a, b, ssq_a, ssq_b, w, residual)


def _mm_fullk_t_kernel(a_ref, ssq_ref, w_ref, o_ref, *, scale, n_scaled):
    s = jnp.where(pl.program_id(1) < n_scaled, jnp.float32(scale), jnp.float32(1.0))
    r = _rstd(ssq_ref, a_ref.shape[1]) * s
    o_ref[...] = (_dot_t(a_ref[...], w_ref[...]) * r).astype(o_ref.dtype)


def matmul_fullk_t(a, ssq, w_t, layer, *, out_dtype, n_cols, scale=1.0, n_scaled=0, tm=2048, tn=512):
    m, kdim = a.shape
    tm = min(tm, m)
    return pl.pallas_call(
        functools.partial(_mm_fullk_t_kernel, scale=scale, n_scaled=n_scaled),
        grid=(m // tm, n_cols // tn),
        in_specs=[_resident_spec(tm, kdim),
                  pl.BlockSpec((tm, 1), lambda i, j: (i, 0)),
                  pl.BlockSpec((None, tn, kdim), lambda i, j: (layer, j, 0))],
        out_specs=pl.BlockSpec((tm, tn), lambda i, j: (i, j)),
        out_shape=jax.ShapeDtypeStruct((m, n_cols), out_dtype),
        compiler_params=_params(2),
        name="matmul_fullk_t",
    )(a, ssq, w_t)


def matmul_fullk(a, w, layer, *, out_dtype, tm=2048, tn=256):
    m, kdim = a.shape
    n = w.shape[-1]
    tn = min(tn, n)
    tm = min(tm, m)
    return pl.pallas_call(
        functools.partial(_mm_fullk_kernel, scale=1.0, n_scaled=0),
        grid=(m // tm, n // tn),
        in_specs=[_resident_spec(tm, kdim), _w_spec(w, layer, (kdim, tn), lambda i, j: (0, j))],
        out_specs=pl.BlockSpec((tm, tn), lambda i, j: (i, j)),
        out_shape=jax.ShapeDtypeStruct((m, n), out_dtype),
        compiler_params=_params(2),
        name="matmul_fullk",
    )(a, w)


def _ffn_up_kernel(a_ref, wg_ref, wu_ref, wd_ref, o_ref, wd_bf16_ref, *, n_chunks):
    step = pl.program_id(0) * pl.num_programs(1) + pl.program_id(1)

    @pl.when(step < n_chunks)
    def _():
        wd_bf16_ref[...] = wd_ref[...].astype(BF16)

    a = a_ref[...]
    g = _dot(a, wg_ref[...])
    u = _dot(a, wu_ref[...])
    o_ref[...] = (g * jax.nn.sigmoid(g) * u).astype(o_ref.dtype)


def ffn_up(a, w_gate, w_up, w_down, layer, tm=2048, tn=256):
    m, kdim = a.shape
    n = w_gate.shape[-1]
    tm = min(tm, m)
    n_j = n // tn
    n_steps = (m // tm) * n_j
    chunk = next(c for c in range(128, n + 1, 128) if n % c == 0 and n // c <= n_steps)
    n_chunks = n // chunk
    d_out = w_down.shape[-1]
    chunk_of = lambda i, j: jnp.minimum(i * n_j + j, n_chunks - 1)
    w_spec = _w_spec(w_gate, layer, (kdim, tn), lambda i, j: (0, j))
    return pl.pallas_call(
        functools.partial(_ffn_up_kernel, n_chunks=n_chunks),
        grid=(m // tm, n_j),
        in_specs=[_resident_spec(tm, kdim), w_spec, w_spec,
                  pl.BlockSpec((None, chunk, d_out), lambda i, j: (layer, chunk_of(i, j), 0))],
        out_specs=[pl.BlockSpec((tm, tn), lambda i, j: (i, j)),
                   pl.BlockSpec((chunk, d_out), lambda i, j: (chunk_of(i, j), 0))],
        out_shape=[jax.ShapeDtypeStruct((m, n), BF16),
                   jax.ShapeDtypeStruct((n, d_out), BF16)],
        compiler_params=_params(2),
        name="ffn_up",
    )(a, w_gate, w_up, w_down)


def _mm_ksplit_res_kernel(a_ref, w_ref, r_ref, *rest, nk, k_rem, scale, emit):
    if emit:
        g_ref, o_ref, og_ref, ssq_ref, acc_ref = rest
    else:
        o_ref, acc_ref = rest
    k = pl.program_id(2)

    @pl.when(k == 0)
    def _():
        acc_ref[...] = _dot(a_ref[...], w_ref[...])

    @pl.when(jnp.logical_and(k > 0, k < nk - 1))
    def _():
        acc_ref[...] += _dot(a_ref[...], w_ref[...])

    @pl.when(k == nk - 1)
    def _():
        tail = _dot(a_ref[:, :k_rem], w_ref[:k_rem, :])
        o = r_ref[...] + scale * (acc_ref[...] + tail)
        o_ref[...] = o
        if emit:
            og_ref[...] = (o * g_ref[...]).astype(og_ref.dtype)
            part = jnp.sum(o * o, axis=-1, keepdims=True)
            j = pl.program_id(1)

            @pl.when(j == 0)
            def _():
                ssq_ref[...] = part

            @pl.when(j > 0)
            def _():
                ssq_ref[...] += part


def matmul_ksplit_res(a, w, residual, scale, g_next=None, tm=1024, tn=1024, tk=2816):
    m, kdim = a.shape
    n = w.shape[-1]
    tn = min(tn, n)
    tk = min(tk, kdim)
    nk = pl.cdiv(kdim, tk)
    assert nk >= 2
    k_rem = kdim - (nk - 1) * tk
    emit = g_next is not None
    o_spec = pl.BlockSpec((tm, tn), lambda i, j, k: (i, j))
    in_specs = [pl.BlockSpec((tm, tk), lambda i, j, k: (i, k)),
                pl.BlockSpec((tk, tn), lambda i, j, k: (k, j)),
                o_spec]
    args = (a, w, residual)
    out_specs, out_shape = o_spec, jax.ShapeDtypeStruct((m, n), F32)
    if emit:
        in_specs.append(pl.BlockSpec((1, tn), lambda i, j, k: (0, j)))
        args += (g_next.reshape(1, n),)
        out_specs = [o_spec, o_spec, pl.BlockSpec((tm, 1), lambda i, j, k: (i, 0))]
        out_shape = [out_shape, jax.ShapeDtypeStruct((m, n), BF16), jax.ShapeDtypeStruct((m, 1), F32)]
    return pl.pallas_call(
        functools.partial(_mm_ksplit_res_kernel, nk=nk, k_rem=k_rem, scale=scale, emit=emit),
        grid=(m // tm, n // tn, nk),
        in_specs=in_specs,
        out_specs=out_specs,
        out_shape=out_shape,
        scratch_shapes=[pltpu.VMEM((tm, tn), F32)],
        compiler_params=_params(3),
        name="matmul_ksplit_res",
    )(*args)


def _rope_slab(x, cos_p, sin_p):
    lane = lax.broadcasted_iota(jnp.int32, x.shape, 1)
    half = ROPE_DIM // 2
    swapped = jnp.where(lane < half, pltpu.roll(x, 128 - half, 1), pltpu.roll(x, half, 1))
    return x * cos_p + swapped * sin_p


def _proj_c_kernel(a_ref, ssq_ref, w_ref, gq_ref, gkv_ref, cos_ref, sin_ref,
                   cq_ref, ckv_ref, kr_ref, acc_ref, *, nk, dim):
    k = pl.program_id(1)

    @pl.when(k == 0)
    def _():
        acc_ref[...] = _dot_t(a_ref[...], w_ref[...])

    @pl.when(k > 0)
    def _():
        acc_ref[...] += _dot_t(a_ref[...], w_ref[...])

    @pl.when(k == nk - 1)
    def _():
        def norm(x, g):
            ms = jnp.mean(x * x, axis=-1, keepdims=True)
            return (x * lax.rsqrt(ms + EPS)) * g
        r = _rstd(ssq_ref, dim)
        cq_ref[...] = norm(acc_ref[:, :Q_LORA] * r, gq_ref[...]).astype(cq_ref.dtype)
        ckv_ref[...] = norm(acc_ref[:, Q_LORA:Q_LORA + KV_LORA] * r, gkv_ref[...]).astype(ckv_ref.dtype)
        kr = acc_ref[:, Q_LORA + KV_LORA:] * r
        kr_ref[...] = _rope_slab(kr, cos_ref[...], sin_ref[...]).astype(kr_ref.dtype)


def proj_c(a, ssq, w_c_t, g_cq, g_ckv, cos_p, sin_p, tm=1024, tk=1024):
    m, kdim = a.shape
    n = w_c_t.shape[0]
    tm = min(tm, m)
    tk = min(tk, kdim)
    nk = kdim // tk
    row = lambda width: pl.BlockSpec((tm, width), lambda i, k: (i, 0))
    vec = lambda width: pl.BlockSpec((1, width), lambda i, k: (0, 0))
    return pl.pallas_call(
        functools.partial(_proj_c_kernel, nk=nk, dim=kdim),
        grid=(m // tm, nk),
        in_specs=[pl.BlockSpec((tm, tk), lambda i, k: (i, k)),
                  row(1),
                  pl.BlockSpec((n, tk), lambda i, k: (0, k)),
                  vec(Q_LORA), vec(KV_LORA), row(128), row(128)],
        out_specs=[row(Q_LORA), row(KV_LORA), row(128)],
        out_shape=[jax.ShapeDtypeStruct((m, Q_LORA), BF16),
                   jax.ShapeDtypeStruct((m, KV_LORA), BF16),
                   jax.ShapeDtypeStruct((m, 128), BF16)],
        scratch_shapes=[pltpu.VMEM((tm, n), F32)],
        compiler_params=_params(2),
        name="proj_c",
    )(a, ssq, w_c_t, g_cq.reshape(1, Q_LORA), g_ckv.reshape(1, KV_LORA), cos_p, sin_p)


def _q_up_kernel(a_ref, w_ref, cos_ref, sin_ref, qn_ref, qr_ref, *, scale):
    x = _dot(a_ref[...], w_ref[...])
    cos_p = cos_ref[...]
    sin_p = sin_ref[...]
    low = lax.broadcasted_iota(jnp.int32, (x.shape[0], 128), 1) < ROPE_DIM
    for pair in range(x.shape[1] // 384):
        v0, v1, v2 = (x[:, (3 * pair + t) * 128:(3 * pair + t + 1) * 128] for t in range(3))
        v1r = pltpu.roll(v1, 64, 1)
        v2r = pltpu.roll(v2, 64, 1)
        heads = ((v0, jnp.where(low, v1, 0.0)),
                 (jnp.where(low, v1r, v2r), jnp.where(low, v2r, 0.0)))
        for t, (nope, rope) in enumerate(heads):
            cols = pl.ds((2 * pair + t) * 128, 128)
            qn_ref[:, cols] = (nope * scale).astype(qn_ref.dtype)
            qr_ref[:, cols] = (_rope_slab(rope, cos_p, sin_p) * scale).astype(qr_ref.dtype)


def q_up(a, w_uq, layer, cos_p, sin_p, scale, tm=2048, heads_per_tile=4):
    m, kdim = a.shape
    tm = min(tm, m)
    tn_in = heads_per_tile * (HEAD_DIM + ROPE_DIM)
    tn_out = heads_per_tile * 128
    o_spec = pl.BlockSpec((tm, tn_out), lambda i, j: (i, j))
    return pl.pallas_call(
        functools.partial(_q_up_kernel, scale=scale),
        grid=(m // tm, N_HEADS // heads_per_tile),
        in_specs=[pl.BlockSpec((tm, kdim), lambda i, j: (i, 0)),
                  pl.BlockSpec((None, kdim, tn_in), lambda i, j: (layer, 0, j)),
                  pl.BlockSpec((tm, 128), lambda i, j: (i, 0)),
                  pl.BlockSpec((tm, 128), lambda i, j: (i, 0))],
        out_specs=[o_spec, o_spec],
        out_shape=[jax.ShapeDtypeStruct((m, WIDTH), BF16), jax.ShapeDtypeStruct((m, WIDTH), BF16)],
        compiler_params=_params(2),
        name="q_up",
    )(a, w_uq, cos_p, sin_p)


def _qk(q, k):
    return lax.dot_general(q, k, (((1,), (1,)), ((), ())), preferred_element_type=F32)


def _softmax_update(s, v, m_ref, l_ref, acc_ref):
    m_prev = m_ref[...]
    m_new = jnp.maximum(m_prev, jnp.max(s, axis=-1, keepdims=True))
    alpha = jnp.exp2(m_prev - m_new)
    p = jnp.exp2(s - jnp.concatenate([m_new] * (s.shape[1] // 128), axis=1))
    l_ref[...] = alpha * l_ref[...] + jnp.sum(p, axis=-1, keepdims=True)
    acc_ref[...] = alpha * acc_ref[...] + jnp.dot(p.astype(BF16), v, preferred_element_type=F32)
    m_ref[...] = m_new


def _softmax_reset(m_ref, l_ref, acc_ref):
    m_ref[...] = jnp.full(m_ref.shape, MASK, F32)
    l_ref[...] = jnp.zeros(l_ref.shape, F32)
    acc_ref[...] = jnp.zeros(acc_ref.shape, F32)


def _softmax_scratch(nq):
    return [pltpu.VMEM((nq, ATT_TQ, 128), F32), pltpu.VMEM((nq, ATT_TQ, 128), F32),
            pltpu.VMEM((nq, ATT_TQ, HEAD_DIM), F32)]


def _head_block(seq, col):
    return pl.BlockSpec((seq, HEAD_DIM), col)


def _emit_head(rows, g_ref, og_ref, ssq_ref, l_ref, acc_ref):
    o = acc_ref[...] / l_ref[...]
    og_ref[rows, :] = (o * g_ref[...]).astype(og_ref.dtype)
    ssq_ref[rows, :] += jnp.sum(o * o, axis=-1, keepdims=True)


def _mla_kernel(qn_ref, qr_ref, kn_ref, kr_ref, v_ref, g_ref, wo_ref, og_ref, ssq_ref, wo_bf16_ref,
                kc_ref, m_ref, l_ref, acc_ref):
    wo_bf16_ref[...] = wo_ref[...].astype(BF16)

    @pl.when(pl.program_id(1) == 0)
    def _():
        ssq_ref[...] = jnp.zeros_like(ssq_ref)

    kc_ref[:, :HEAD_DIM] = kn_ref[...]
    kc_ref[:, HEAD_DIM:] = kr_ref[...]
    seq = og_ref.shape[0]
    row = lax.broadcasted_iota(jnp.int32, (ATT_TQ, ATT_TK), 0)
    col = lax.broadcasted_iota(jnp.int32, (ATT_TQ, ATT_TK), 1)
    for qi in range(seq // ATT_TQ):
        rows = pl.ds(qi * ATT_TQ, ATT_TQ)
        q = jnp.concatenate([qn_ref[rows, :], qr_ref[rows, :]], axis=-1)
        state = (m_ref.at[qi], l_ref.at[qi], acc_ref.at[qi])
        _softmax_reset(*state)
        for j in range(qi + 1):
            keys = pl.ds(j * ATT_TK, ATT_TK)
            s = _qk(q, kc_ref[keys, :])
            if j == qi:
                s = jnp.where(row >= col, s, MASK)
            _softmax_update(s, v_ref[keys, :], *state)
        _emit_head(rows, g_ref, og_ref, ssq_ref, l_ref.at[qi], acc_ref.at[qi])


def _group_out(batch, seq):
    specs = [_head_block(seq, lambda b, h: (b, h)), pl.BlockSpec((seq, 1), lambda b, h: (b, 0))]
    shapes = [jax.ShapeDtypeStruct((batch * seq, WIDTH), BF16),
              jax.ShapeDtypeStruct((batch * seq, 1), F32)]
    return specs, shapes


def mla_attention(q_nope, q_rope, kv, k_r, g_out, w_o, layer, batch, seq):
    out_specs, out_shape = _group_out(batch, seq)
    k_o, n_o = w_o.shape[1:]
    chunk = k_o // (batch * N_HEADS)
    step = lambda b, h: b * N_HEADS + h
    out_specs = out_specs + [pl.BlockSpec((chunk, n_o), lambda b, h: (step(b, h), 0))]
    out_shape = out_shape + [jax.ShapeDtypeStruct((k_o, n_o), BF16)]
    return pl.pallas_call(
        _mla_kernel,
        grid=(batch, N_HEADS),
        in_specs=[_head_block(seq, lambda b, h: (b, h)),
                  _head_block(seq, lambda b, h: (b, h)),
                  _head_block(seq, lambda b, h: (b, 2 * h)),
                  _head_block(seq, lambda b, h: (b, 0)),
                  _head_block(seq, lambda b, h: (b, 2 * h + 1)),
                  pl.BlockSpec((1, HEAD_DIM), lambda b, h: (0, h)),
                  pl.BlockSpec((None, chunk, n_o), lambda b, h: (layer, step(b, h), 0))],
        out_specs=out_specs,
        out_shape=out_shape,
        scratch_shapes=[pltpu.VMEM((seq, 2 * HEAD_DIM), BF16)] + _softmax_scratch(seq // ATT_TQ),
        compiler_params=_params(2),
        name="mla_attention",
    )(q_nope, q_rope, kv, k_r, kv, g_out.reshape(1, WIDTH), w_o)


def _dilated_tables(off):
    row = lax.broadcasted_iota(jnp.int32, (ATT_TQ, ATT_TK), 0)
    col = lax.broadcasted_iota(jnp.int32, (ATT_TQ, ATT_TK), 1)
    dist = row - col + off * ATT_TK
    mult = jnp.zeros((ATT_TQ, ATT_TK), jnp.int32)
    for window, dilation in DIL_CONFIGS:
        assert dilation & (dilation - 1) == 0
        held = (dist >= 0) & (dist <= window) & ((dist & (dilation - 1)) == 0)
        mult = mult + held.astype(jnp.int32)
    log2_mult = jnp.full((ATT_TQ, ATT_TK), MASK, F32)
    for n in range(1, len(DIL_CONFIGS) + 1):
        log2_mult = jnp.where(mult == n, jnp.float32(math.log2(n)), log2_mult)
    return dist.astype(F32), log2_mult


def _dilated_kernel(slope_ref, q_ref, k_ref, v_ref, g_ref, og_ref, ssq_ref, dist_ref, lmult_ref, bias_ref,
                    m_ref, l_ref, acc_ref):
    @pl.when(jnp.logical_and(pl.program_id(0) == 0, pl.program_id(1) == 0))
    def _():
        for off in range(DIL_NOFF):
            dist_ref[off], lmult_ref[off] = _dilated_tables(off)

    @pl.when(pl.program_id(1) == 0)
    def _():
        ssq_ref[...] = jnp.zeros_like(ssq_ref)

    neg_slope = -slope_ref[pl.program_id(1)] * LOG2E
    for off in range(DIL_NOFF):
        bias_ref[off] = neg_slope * dist_ref[off] + lmult_ref[off]
    seq = og_ref.shape[0]
    for qi in range(seq // ATT_TQ):
        rows = pl.ds(qi * ATT_TQ, ATT_TQ)
        q = q_ref[rows, :]
        state = (m_ref.at[qi], l_ref.at[qi], acc_ref.at[qi])
        _softmax_reset(*state)
        for off in range(min(qi, DIL_NOFF - 1), -1, -1):
            keys = pl.ds((qi - off) * ATT_TK, ATT_TK)
            s = _qk(q, k_ref[keys, :]) + bias_ref[off]
            _softmax_update(s, v_ref[keys, :], *state)
        _emit_head(rows, g_ref, og_ref, ssq_ref, l_ref.at[qi], acc_ref.at[qi])


def dilated_attention(qkv, slopes, g_out, batch, seq):
    out_specs, out_shape = _group_out(batch, seq)
    return pl.pallas_call(
        _dilated_kernel,
        grid=(batch, N_HEADS),
        in_specs=[pl.BlockSpec(memory_space=pltpu.SMEM),
                  _head_block(seq, lambda b, h: (b, h)),
                  _head_block(seq, lambda b, h: (b, N_HEADS + h)),
                  _head_block(seq, lambda b, h: (b, 2 * N_HEADS + h)),
                  pl.BlockSpec((1, HEAD_DIM), lambda b, h: (0, h))],
        out_specs=out_specs,
        out_shape=out_shape,
        scratch_shapes=[pltpu.VMEM((DIL_NOFF, ATT_TQ, ATT_TK), F32)] * 3 + _softmax_scratch(seq // ATT_TQ),
        compiler_params=_params(2),
        name="dilated_attention",
    )(slopes, qkv, qkv, qkv, g_out.reshape(1, WIDTH))


def kernel(x, positions, ffn1_norm, ffn1_w_gate, ffn1_w_up, ffn1_w_down, mix_norm, w_in, g_cq, w_uq,
           g_ckv, w_ukv, g_out_a, g_out_b, w_o, ffn2_norm, ffn2_w_gate, ffn2_w_up, ffn2_w_down,
           final_norm):
    batch, seq, d = x.shape
    m = batch * seq
    depth = w_in.shape[0]

    inv_freq = 1.0 / (ROPE_THETA ** (jnp.arange(0, ROPE_DIM, 2, dtype=F32) / ROPE_DIM))
    ang = positions.astype(F32)[..., None] * inv_freq
    cos = jnp.cos(ang).reshape(m, ROPE_DIM // 2)
    sin = jnp.sin(ang).reshape(m, ROPE_DIM // 2)
    zeros = jnp.zeros((m, 128 - ROPE_DIM), F32)
    cos_p = jnp.concatenate([cos, cos, zeros], axis=-1)
    sin_p = jnp.concatenate([-sin, sin, zeros], axis=-1)

    slopes = jnp.asarray(
        np.array([2.0 ** (-8.0 * (i + 1) / N_HEADS) for i in range(N_HEADS)], dtype=np.float32))
    scale_a = HEAD_DIM ** -0.5 * LOG2E
    scale_b = (HEAD_DIM + ROPE_DIM) ** -0.5 * LOG2E

    w_in_t = jnp.swapaxes(w_in, 1, 2)

    h = x.reshape(m, d)
    for l in range(depth):
        n = rmsnorm(h, ffn1_norm[l], BF16)
        hid, w_down = ffn_up(n, ffn1_w_gate, ffn1_w_up, ffn1_w_down, l)
        h, hg, ssq = matmul_ksplit_res(hid, w_down, h, 0.5, g_next=mix_norm[l])

        qkv_a = matmul_fullk_t(hg, ssq, w_in_t, l, out_dtype=BF16, n_cols=3 * WIDTH, scale=scale_a,
                               n_scaled=WIDTH // 512)
        w_c_t = jnp.concatenate(
            [w_in_t[l, 3 * WIDTH:, :], jnp.zeros((128 - ROPE_DIM, d), F32)], axis=0)
        cq_n, ckv_n, k_r = proj_c(hg, ssq, w_c_t, g_cq[l], g_ckv[l], cos_p, sin_p)

        out_a, ssq_a = dilated_attention(qkv_a, slopes, g_out_a[l], batch, seq)

        q_nope, q_rope = q_up(cq_n, w_uq, l, cos_p, sin_p, scale_b)
        kv = matmul_fullk(ckv_n, w_ukv, l, out_dtype=BF16, tn=1024)
        out_b, ssq_b, w_o_bf16 = mla_attention(q_nope, q_rope, kv, k_r, g_out_b[l], w_o, l, batch, seq)

        h = out_proj(out_a, out_b, ssq_a, ssq_b, w_o_bf16, None, h)

        n = rmsnorm(h, ffn2_norm[l], BF16)
        hid, w_down = ffn_up(n, ffn2_w_gate, ffn2_w_up, ffn2_w_down, l)
        h = matmul_ksplit_res(hid, w_down, h, 0.5)

    return rmsnorm(h, final_norm, F32).reshape(batch, seq, d)
```

```python
import functools
import math

import numpy as np
import jax
import jax.numpy as jnp
from jax import lax
from jax.experimental import pallas as pl
from jax.experimental.pallas import tpu as pltpu

F32 = jnp.float32
BF16 = jnp.bfloat16

HEAD_DIM = 128
N_HEADS = 16
WIDTH = N_HEADS * HEAD_DIM
DIL_CONFIGS = ((128, 1), (512, 4), (2048, 16))
ROPE_DIM = 64
Q_LORA = 1024
KV_LORA = 512
ROPE_THETA = 10000.0
EPS = 1e-6
MASK = -1e30
LOG2E = math.log2(math.e)

VMEM_LIMIT = 56 * 1024 * 1024
ATT_TQ = 512
ATT_TK = 512
ATT_KCHUNK = 2 * ATT_TK
DIL_MAX_DIST = max(w for w, _ in DIL_CONFIGS)
DIL_NOFF = DIL_MAX_DIST // ATT_TK + 1


def _params(n_axes):
    return pltpu.CompilerParams(
        dimension_semantics=("arbitrary",) * n_axes, vmem_limit_bytes=VMEM_LIMIT)


def _rmsnorm_kernel(x_ref, g_ref, o_ref):
    x = x_ref[...]
    ms = jnp.mean(x * x, axis=-1, keepdims=True)
    o_ref[...] = ((x * lax.rsqrt(ms + EPS)) * g_ref[...]).astype(o_ref.dtype)


def rmsnorm(x, g, out_dtype, tm=512):
    m, d = x.shape
    return pl.pallas_call(
        _rmsnorm_kernel,
        grid=(m // tm,),
        in_specs=[pl.BlockSpec((tm, d), lambda i: (i, 0)),
                  pl.BlockSpec((1, d), lambda i: (0, 0))],
        out_specs=pl.BlockSpec((tm, d), lambda i: (i, 0)),
        out_shape=jax.ShapeDtypeStruct((m, d), out_dtype),
        compiler_params=_params(1),
        name="rmsnorm",
    )(x, g.reshape(1, d))


def _rstd(ssq_ref, dim):
    return lax.rsqrt(ssq_ref[...] * (1.0 / dim) + EPS)


def _dot(a, w):
    return jnp.dot(a, w.astype(BF16), preferred_element_type=F32)


def _dot_t(a, w_t):
    return lax.dot_general(a, w_t.astype(BF16), (((1,), (1,)), ((), ())), preferred_element_type=F32)


def _w_spec(w, layer, block, index_map):
    if w.ndim == 3:
        return pl.BlockSpec((None,) + block, lambda *g: (layer,) + index_map(*g))
    return pl.BlockSpec(block, index_map)


def _resident_spec(tm, kdim):
    return pl.BlockSpec((tm, kdim), lambda i, j: (i, 0), pipeline_mode=pl.Buffered(1))


def _mm_fullk_kernel(a_ref, w_ref, o_ref, *, scale, n_scaled):
    s = jnp.where(pl.program_id(1) < n_scaled, jnp.float32(scale), jnp.float32(1.0))
    o_ref[...] = (_dot(a_ref[...], w_ref[...]) * s).astype(o_ref.dtype)


def _out_proj_kernel(a_ref, b_ref, ssq_a_ref, ssq_b_ref, w_ref, r_ref, o_ref):
    ka = a_ref.shape[1]
    ya = _dot(a_ref[...], w_ref[:ka, :]) * _rstd(ssq_a_ref, ka)
    yb = _dot(b_ref[...], w_ref[ka:, :]) * _rstd(ssq_b_ref, b_ref.shape[1])
    o_ref[...] = r_ref[...] + ya + yb


def out_proj(a, b, ssq_a, ssq_b, w, layer, residual, tm=1024, tn=512):
    m, ka = a.shape
    kb = b.shape[1]
    n = w.shape[-1]
    tn = min(tn, n)
    tm = min(tm, m)
    rows = lambda width: pl.BlockSpec((tm, width), lambda i, j: (i, 0))
    o_spec = pl.BlockSpec((tm, tn), lambda i, j: (i, j))
    return pl.pallas_call(
        _out_proj_kernel,
        grid=(m // tm, n // tn),
        in_specs=[rows(ka), rows(kb), rows(1), rows(1),
                  _w_spec(w, layer, (ka + kb, tn), lambda i, j: (0, j)), o_spec],
        out_specs=o_spec,
        out_shape=jax.ShapeDtypeStruct((m, n), F32),
        compiler_params=_params(2),
        name="out_proj",
    )(a, b, ssq_a, ssq_b, w, residual)


def _mm_fullk_t_kernel(a_ref, ssq_ref, w_ref, o_ref, *, scale, n_scaled):
    s = jnp.where(pl.program_id(1) < n_scaled, jnp.float32(scale), jnp.float32(1.0))
    r = _rstd(ssq_ref, a_ref.shape[1]) * s
    o_ref[...] = (_dot_t(a_ref[...], w_ref[...]) * r).astype(o_ref.dtype)


def matmul_fullk_t(a, ssq, w_t, layer, *, out_dtype, n_cols, scale=1.0, n_scaled=0, tm=2048, tn=512):
    m, kdim = a.shape
    tm = min(tm, m)
    return pl.pallas_call(
        functools.partial(_mm_fullk_t_kernel, scale=scale, n_scaled=n_scaled),
        grid=(m // tm, n_cols // tn),
        in_specs=[_resident_spec(tm, kdim),
                  pl.BlockSpec((tm, 1), lambda i, j: (i, 0)),
                  pl.BlockSpec((None, tn, kdim), lambda i, j: (layer, j, 0))],
        out_specs=pl.BlockSpec((tm, tn), lambda i, j: (i, j)),
        out_shape=jax.ShapeDtypeStruct((m, n_cols), out_dtype),
        compiler_params=_params(2),
        name="matmul_fullk_t",
    )(a, ssq, w_t)


def matmul_fullk(a, w, layer, *, out_dtype, tm=2048, tn=256):
    m, kdim = a.shape
    n = w.shape[-1]
    tn = min(tn, n)
    tm = min(tm, m)
    return pl.pallas_call(
        functools.partial(_mm_fullk_kernel, scale=1.0, n_scaled=0),
        grid=(m // tm, n // tn),
        in_specs=[_resident_spec(tm, kdim), _w_spec(w, layer, (kdim, tn), lambda i, j: (0, j))],
        out_specs=pl.BlockSpec((tm, tn), lambda i, j: (i, j)),
        out_shape=jax.ShapeDtypeStruct((m, n), out_dtype),
        compiler_params=_params(2),
        name="matmul_fullk",
    )(a, w)


def _ffn_up_kernel(a_ref, wg_ref, wu_ref, wd_ref, o_ref, wd_bf16_ref, *, n_chunks):
    step = pl.program_id(0) * pl.num_programs(1) + pl.program_id(1)

    @pl.when(step < n_chunks)
    def _():
        wd_bf16_ref[...] = wd_ref[...].astype(BF16)

    a = a_ref[...]
    g = _dot(a, wg_ref[...])
    u = _dot(a, wu_ref[...])
    o_ref[...] = (g * jax.nn.sigmoid(g) * u).astype(o_ref.dtype)


def ffn_up(a, w_gate, w_up, w_down, layer, tm=2048, tn=256):
    m, kdim = a.shape
    n = w_gate.shape[-1]
    tm = min(tm, m)
    n_j = n // tn
    n_steps = (m // tm) * n_j
    chunk = next(c for c in range(128, n + 1, 128) if n % c == 0 and n // c <= n_steps)
    n_chunks = n // chunk
    d_out = w_down.shape[-1]
    chunk_of = lambda i, j: jnp.minimum(i * n_j + j, n_chunks - 1)
    w_spec = _w_spec(w_gate, layer, (kdim, tn), lambda i, j: (0, j))
    return pl.pallas_call(
        functools.partial(_ffn_up_kernel, n_chunks=n_chunks),
        grid=(m // tm, n_j),
        in_specs=[_resident_spec(tm, kdim), w_spec, w_spec,
                  pl.BlockSpec((None, chunk, d_out), lambda i, j: (layer, chunk_of(i, j), 0))],
        out_specs=[pl.BlockSpec((tm, tn), lambda i, j: (i, j)),
                   pl.BlockSpec((chunk, d_out), lambda i, j: (chunk_of(i, j), 0))],
        out_shape=[jax.ShapeDtypeStruct((m, n), BF16),
                   jax.ShapeDtypeStruct((n, d_out), BF16)],
        compiler_params=_params(2),
        name="ffn_up",
    )(a, w_gate, w_up, w_down)


def _mm_ksplit_res_kernel(a_ref, w_ref, r_ref, *rest, nk, k_rem, scale, emit):
    if emit:
        g_ref, o_ref, og_ref, ssq_ref, acc_ref = rest
    else:
        o_ref, acc_ref = rest
    k = pl.program_id(2)

    @pl.when(k == 0)
    def _():
        acc_ref[...] = _dot(a_ref[...], w_ref[...])

    @pl.when(jnp.logical_and(k > 0, k < nk - 1))
    def _():
        acc_ref[...] += _dot(a_ref[...], w_ref[...])

    @pl.when(k == nk - 1)
    def _():
        tail = _dot(a_ref[:, :k_rem], w_ref[:k_rem, :])
        o = r_ref[...] + scale * (acc_ref[...] + tail)
        o_ref[...] = o
        if emit:
            og_ref[...] = (o * g_ref[...]).astype(og_ref.dtype)
            part = jnp.sum(o * o, axis=-1, keepdims=True)
            j = pl.program_id(1)

            @pl.when(j == 0)
            def _():
                ssq_ref[...] = part

            @pl.when(j > 0)
            def _():
                ssq_ref[...] += part


def matmul_ksplit_res(a, w, residual, scale, g_next=None, tm=1024, tn=1024, tk=2816):
    m, kdim = a.shape
    n = w.shape[-1]
    tn = min(tn, n)
    tk = min(tk, kdim)
    nk = pl.cdiv(kdim, tk)
    assert nk >= 2
    k_rem = kdim - (nk - 1) * tk
    emit = g_next is not None
    o_spec = pl.BlockSpec((tm, tn), lambda i, j, k: (i, j))
    in_specs = [pl.BlockSpec((tm, tk), lambda i, j, k: (i, k)),
                pl.BlockSpec((tk, tn), lambda i, j, k: (k, j)),
                o_spec]
    args = (a, w, residual)
    out_specs, out_shape = o_spec, jax.ShapeDtypeStruct((m, n), F32)
    if emit:
        in_specs.append(pl.BlockSpec((1, tn), lambda i, j, k: (0, j)))
        args += (g_next.reshape(1, n),)
        out_specs = [o_spec, o_spec, pl.BlockSpec((tm, 1), lambda i, j, k: (i, 0))]
        out_shape = [out_shape, jax.ShapeDtypeStruct((m, n), BF16), jax.ShapeDtypeStruct((m, 1), F32)]
    return pl.pallas_call(
        functools.partial(_mm_ksplit_res_kernel, nk=nk, k_rem=k_rem, scale=scale, emit=emit),
        grid=(m // tm, n // tn, nk),
        in_specs=in_specs,
        out_specs=out_specs,
        out_shape=out_shape,
        scratch_shapes=[pltpu.VMEM((tm, tn), F32)],
        compiler_params=_params(3),
        name="matmul_ksplit_res",
    )(*args)


def _rope_slab(x, cos_p, sin_p):
    lane = lax.broadcasted_iota(jnp.int32, x.shape, 1)
    half = ROPE_DIM // 2
    swapped = jnp.where(lane < half, pltpu.roll(x, 128 - half, 1), pltpu.roll(x, half, 1))
    return x * cos_p + swapped * sin_p


def _proj_c_kernel(a_ref, ssq_ref, w_ref, gq_ref, gkv_ref, cos_ref, sin_ref,
                   cq_ref, ckv_ref, kr_ref, acc_ref, *, nk, dim):
    k = pl.program_id(1)

    @pl.when(k == 0)
    def _():
        acc_ref[...] = _dot_t(a_ref[...], w_ref[...])

    @pl.when(k > 0)
    def _():
        acc_ref[...] += _dot_t(a_ref[...], w_ref[...])

    @pl.when(k == nk - 1)
    def _():
        def norm(x, g):
            ms = jnp.mean(x * x, axis=-1, keepdims=True)
            return (x * lax.rsqrt(ms + EPS)) * g
        r = _rstd(ssq_ref, dim)
        cq_ref[...] = norm(acc_ref[:, :Q_LORA] * r, gq_ref[...]).astype(cq_ref.dtype)
        ckv_ref[...] = norm(acc_ref[:, Q_LORA:Q_LORA + KV_LORA] * r, gkv_ref[...]).astype(ckv_ref.dtype)
        kr = acc_ref[:, Q_LORA + KV_LORA:] * r
        kr_ref[...] = _rope_slab(kr, cos_ref[...], sin_ref[...]).astype(kr_ref.dtype)


def proj_c(a, ssq, w_c_t, g_cq, g_ckv, cos_p, sin_p, tm=1024, tk=1024):
    m, kdim = a.shape
    n = w_c_t.shape[0]
    tm = min(tm, m)
    tk = min(tk, kdim)
    nk = kdim // tk
    row = lambda width: pl.BlockSpec((tm, width), lambda i, k: (i, 0))
    vec = lambda width: pl.BlockSpec((1, width), lambda i, k: (0, 0))
    return pl.pallas_call(
        functools.partial(_proj_c_kernel, nk=nk, dim=kdim),
        grid=(m // tm, nk),
        in_specs=[pl.BlockSpec((tm, tk), lambda i, k: (i, k)),
                  row(1),
                  pl.BlockSpec((n, tk), lambda i, k: (0, k)),
                  vec(Q_LORA), vec(KV_LORA), row(128), row(128)],
        out_specs=[row(Q_LORA), row(KV_LORA), row(128)],
        out_shape=[jax.ShapeDtypeStruct((m, Q_LORA), BF16),
                   jax.ShapeDtypeStruct((m, KV_LORA), BF16),
                   jax.ShapeDtypeStruct((m, 128), BF16)],
        scratch_shapes=[pltpu.VMEM((tm, n), F32)],
        compiler_params=_params(2),
        name="proj_c",
    )(a, ssq, w_c_t, g_cq.reshape(1, Q_LORA), g_ckv.reshape(1, KV_LORA), cos_p, sin_p)


def _q_up_kernel(a_ref, w_ref, cos_ref, sin_ref, qn_ref, qr_ref, *, scale):
    x = _dot(a_ref[...], w_ref[...])
    cos_p = cos_ref[...]
    sin_p = sin_ref[...]
    low = lax.broadcasted_iota(jnp.int32, (x.shape[0], 128), 1) < ROPE_DIM
    for pair in range(x.shape[1] // 384):
        v0, v1, v2 = (x[:, (3 * pair + t) * 128:(3 * pair + t + 1) * 128] for t in range(3))
        v1r = pltpu.roll(v1, 64, 1)
        v2r = pltpu.roll(v2, 64, 1)
        heads = ((v0, jnp.where(low, v1, 0.0)),
                 (jnp.where(low, v1r, v2r), jnp.where(low, v2r, 0.0)))
        for t, (nope, rope) in enumerate(heads):
            cols = pl.ds((2 * pair + t) * 128, 128)
            qn_ref[:, cols] = (nope * scale).astype(qn_ref.dtype)
            qr_ref[:, cols] = (_rope_slab(rope, cos_p, sin_p) * scale).astype(qr_ref.dtype)


def q_up(a, w_uq, layer, cos_p, sin_p, scale, tm=2048, heads_per_tile=4):
    m, kdim = a.shape
    tm = min(tm, m)
    tn_in = heads_per_tile * (HEAD_DIM + ROPE_DIM)
    tn_out = heads_per_tile * 128
    o_spec = pl.BlockSpec((tm, tn_out), lambda i, j: (i, j))
    return pl.pallas_call(
        functools.partial(_q_up_kernel, scale=scale),
        grid=(m // tm, N_HEADS // heads_per_tile),
        in_specs=[pl.BlockSpec((tm, kdim), lambda i, j: (i, 0)),
                  pl.BlockSpec((None, kdim, tn_in), lambda i, j: (layer, 0, j)),
                  pl.BlockSpec((tm, 128), lambda i, j: (i, 0)),
                  pl.BlockSpec((tm, 128), lambda i, j: (i, 0))],
        out_specs=[o_spec, o_spec],
        out_shape=[jax.ShapeDtypeStruct((m, WIDTH), BF16), jax.ShapeDtypeStruct((m, WIDTH), BF16)],
        compiler_params=_params(2),
        name="q_up",
    )(a, w_uq, cos_p, sin_p)


def _qk(q, k):
    return lax.dot_general(q, k, (((1,), (1,)), ((), ())), preferred_element_type=F32)


def _softmax_update(s, v, m_ref, l_ref, acc_ref):
    m_prev = m_ref[...]
    m_new = jnp.maximum(m_prev, jnp.max(s, axis=-1, keepdims=True))
    alpha = jnp.exp2(m_prev - m_new)
    p = jnp.exp2(s - jnp.concatenate([m_new] * (s.shape[1] // 128), axis=1))
    l_ref[...] = alpha * l_ref[...] + jnp.sum(p, axis=-1, keepdims=True)
    acc_ref[...] = alpha * acc_ref[...] + jnp.dot(p.astype(BF16), v, preferred_element_type=F32)
    m_ref[...] = m_new


def _softmax_reset(m_ref, l_ref, acc_ref):
    m_ref[...] = jnp.full(m_ref.shape, MASK, F32)
    l_ref[...] = jnp.zeros(l_ref.shape, F32)
    acc_ref[...] = jnp.zeros(acc_ref.shape, F32)


def _softmax_scratch(nq):
    return [pltpu.VMEM((nq, ATT_TQ, 128), F32), pltpu.VMEM((nq, ATT_TQ, 128), F32),
            pltpu.VMEM((nq, ATT_TQ, HEAD_DIM), F32)]


def _head_block(seq, col):
    return pl.BlockSpec((seq, HEAD_DIM), col)


def _emit_head(rows, g_ref, og_ref, ssq_ref, l_ref, acc_ref):
    o = acc_ref[...] / l_ref[...]
    og_ref[rows, :] = (o * g_ref[...]).astype(og_ref.dtype)
    ssq_ref[rows, :] += jnp.sum(o * o, axis=-1, keepdims=True)


def _mla_kernel(qn_ref, qr_ref, kn_ref, kr_ref, v_ref, g_ref, og_ref, ssq_ref,
                kc_ref, m_ref, l_ref, acc_ref):
    @pl.when(pl.program_id(1) == 0)
    def _():
        ssq_ref[...] = jnp.zeros_like(ssq_ref)

    kc_ref[:, :HEAD_DIM] = kn_ref[...]
    kc_ref[:, HEAD_DIM:] = kr_ref[...]
    seq = og_ref.shape[0]
    for qi in range(seq // ATT_TQ):
        rows = pl.ds(qi * ATT_TQ, ATT_TQ)
        q = jnp.concatenate([qn_ref[rows, :], qr_ref[rows, :]], axis=-1)
        state = (m_ref.at[qi], l_ref.at[qi], acc_ref.at[qi])
        _softmax_reset(*state)
        n_keys = (qi + 1) * ATT_TK
        for k0 in range(0, n_keys, ATT_KCHUNK):
            width = min(ATT_KCHUNK, n_keys - k0)
            keys = pl.ds(k0, width)
            s = _qk(q, kc_ref[keys, :])
            if k0 + width == n_keys:
                row = lax.broadcasted_iota(jnp.int32, (ATT_TQ, width), 0) + qi * ATT_TQ
                col = lax.broadcasted_iota(jnp.int32, (ATT_TQ, width), 1) + k0
                s = jnp.where(row >= col, s, MASK)
            _softmax_update(s, v_ref[keys, :], *state)
        _emit_head(rows, g_ref, og_ref, ssq_ref, l_ref.at[qi], acc_ref.at[qi])


def _group_out(batch, seq):
    specs = [_head_block(seq, lambda b, h: (b, h)), pl.BlockSpec((seq, 1), lambda b, h: (b, 0))]
    shapes = [jax.ShapeDtypeStruct((batch * seq, WIDTH), BF16),
              jax.ShapeDtypeStruct((batch * seq, 1), F32)]
    return specs, shapes


def mla_attention(q_nope, q_rope, kv, k_r, g_out, batch, seq):
    out_specs, out_shape = _group_out(batch, seq)
    return pl.pallas_call(
        _mla_kernel,
        grid=(batch, N_HEADS),
        in_specs=[_head_block(seq, lambda b, h: (b, h)),
                  _head_block(seq, lambda b, h: (b, h)),
                  _head_block(seq, lambda b, h: (b, 2 * h)),
                  _head_block(seq, lambda b, h: (b, 0)),
                  _head_block(seq, lambda b, h: (b, 2 * h + 1)),
                  pl.BlockSpec((1, HEAD_DIM), lambda b, h: (0, h))],
        out_specs=out_specs,
        out_shape=out_shape,
        scratch_shapes=[pltpu.VMEM((seq, 2 * HEAD_DIM), BF16)] + _softmax_scratch(seq // ATT_TQ),
        compiler_params=_params(2),
        name="mla_attention",
    )(q_nope, q_rope, kv, k_r, kv, g_out.reshape(1, WIDTH))


def _dilated_distance(off):
    row = lax.broadcasted_iota(jnp.int32, (ATT_TQ, ATT_TK), 0)
    col = lax.broadcasted_iota(jnp.int32, (ATT_TQ, ATT_TK), 1)
    return row - col + off * ATT_TK


def _dilated_log2_mult(off):
    dist = _dilated_distance(off)
    mult = jnp.zeros((ATT_TQ, ATT_TK), jnp.int32)
    for window, dilation in DIL_CONFIGS:
        assert dilation & (dilation - 1) == 0
        held = (dist >= 0) & (dist <= window) & ((dist & (dilation - 1)) == 0)
        mult = mult + held.astype(jnp.int32)
    log2_mult = jnp.full((ATT_TQ, ATT_TK), MASK, F32)
    for n in range(1, len(DIL_CONFIGS) + 1):
        log2_mult = jnp.where(mult == n, jnp.float32(math.log2(n)), log2_mult)
    return log2_mult


def _dilated_kernel(slope_ref, q_ref, k_ref, v_ref, g_ref, og_ref, ssq_ref, lmult_ref, bias_ref,
                    m_ref, l_ref, acc_ref):
    @pl.when(jnp.logical_and(pl.program_id(0) == 0, pl.program_id(1) == 0))
    def _():
        for c in range(DIL_NOFF):
            lmult_ref[:, pl.ds(c * ATT_TK, ATT_TK)] = _dilated_log2_mult(DIL_NOFF - 1 - c)

    @pl.when(pl.program_id(1) == 0)
    def _():
        ssq_ref[...] = jnp.zeros_like(ssq_ref)

    neg_slope = -slope_ref[pl.program_id(1)] * LOG2E
    for c in range(DIL_NOFF):
        cols = pl.ds(c * ATT_TK, ATT_TK)
        bias_ref[:, cols] = neg_slope * _dilated_distance(DIL_NOFF - 1 - c).astype(F32) + lmult_ref[:, cols]
    seq = og_ref.shape[0]
    for qi in range(seq // ATT_TQ):
        rows = pl.ds(qi * ATT_TQ, ATT_TQ)
        q = q_ref[rows, :]
        state = (m_ref.at[qi], l_ref.at[qi], acc_ref.at[qi])
        _softmax_reset(*state)
        n_far = min(qi, DIL_NOFF - 1)
        n_keys = (n_far + 1) * ATT_TK
        for c0 in range(0, n_keys, ATT_KCHUNK):
            width = min(ATT_KCHUNK, n_keys - c0)
            keys = pl.ds((qi - n_far) * ATT_TK + c0, width)
            bias = bias_ref[:, pl.ds((DIL_NOFF - 1 - n_far) * ATT_TK + c0, width)]
            _softmax_update(_qk(q, k_ref[keys, :]) + bias, v_ref[keys, :], *state)
        _emit_head(rows, g_ref, og_ref, ssq_ref, l_ref.at[qi], acc_ref.at[qi])


def dilated_attention(qkv, slopes, g_out, batch, seq):
    out_specs, out_shape = _group_out(batch, seq)
    return pl.pallas_call(
        _dilated_kernel,
        grid=(batch, N_HEADS),
        in_specs=[pl.BlockSpec(memory_space=pltpu.SMEM),
                  _head_block(seq, lambda b, h: (b, h)),
                  _head_block(seq, lambda b, h: (b, N_HEADS + h)),
                  _head_block(seq, lambda b, h: (b, 2 * N_HEADS + h)),
                  pl.BlockSpec((1, HEAD_DIM), lambda b, h: (0, h))],
        out_specs=out_specs,
        out_shape=out_shape,
        scratch_shapes=[pltpu.VMEM((ATT_TQ, DIL_NOFF * ATT_TK), F32)] * 2 + _softmax_scratch(seq // ATT_TQ),
        compiler_params=_params(2),
        name="dilated_attention",
    )(slopes, qkv, qkv, qkv, g_out.reshape(1, WIDTH))


def kernel(x, positions, ffn1_norm, ffn1_w_gate, ffn1_w_up, ffn1_w_down, mix_norm, w_in, g_cq, w_uq,
           g_ckv, w_ukv, g_out_a, g_out_b, w_o, ffn2_norm, ffn2_w_gate, ffn2_w_up, ffn2_w_down,
           final_norm):
    batch, seq, d = x.shape
    m = batch * seq
    depth = w_in.shape[0]

    inv_freq = 1.0 / (ROPE_THETA ** (jnp.arange(0, ROPE_DIM, 2, dtype=F32) / ROPE_DIM))
    ang = positions.astype(F32)[..., None] * inv_freq
    cos = jnp.cos(ang).reshape(m, ROPE_DIM // 2)
    sin = jnp.sin(ang).reshape(m, ROPE_DIM // 2)
    zeros = jnp.zeros((m, 128 - ROPE_DIM), F32)
    cos_p = jnp.concatenate([cos, cos, zeros], axis=-1)
    sin_p = jnp.concatenate([-sin, sin, zeros], axis=-1)

    slopes = jnp.asarray(
        np.array([2.0 ** (-8.0 * (i + 1) / N_HEADS) for i in range(N_HEADS)], dtype=np.float32))
    scale_a = HEAD_DIM ** -0.5 * LOG2E
    scale_b = (HEAD_DIM + ROPE_DIM) ** -0.5 * LOG2E

    w_in_t = jnp.swapaxes(w_in, 1, 2)

    h = x.reshape(m, d)
    for l in range(depth):
        n = rmsnorm(h, ffn1_norm[l], BF16)
        hid, w_down = ffn_up(n, ffn1_w_gate, ffn1_w_up, ffn1_w_down, l)
        h, hg, ssq = matmul_ksplit_res(hid, w_down, h, 0.5, g_next=mix_norm[l])

        qkv_a = matmul_fullk_t(hg, ssq, w_in_t, l, out_dtype=BF16, n_cols=3 * WIDTH, scale=scale_a,
                               n_scaled=WIDTH // 512)
        w_c_t = jnp.concatenate(
            [w_in_t[l, 3 * WIDTH:, :], jnp.zeros((128 - ROPE_DIM, d), F32)], axis=0)
        cq_n, ckv_n, k_r = proj_c(hg, ssq, w_c_t, g_cq[l], g_ckv[l], cos_p, sin_p)

        out_a, ssq_a = dilated_attention(qkv_a, slopes, g_out_a[l], batch, seq)

        q_nope, q_rope = q_up(cq_n, w_uq, l, cos_p, sin_p, scale_b)
        kv = matmul_fullk(ckv_n, w_ukv, l, out_dtype=BF16, tn=1024)
        out_b, ssq_b = mla_attention(q_nope, q_rope, kv, k_r, g_out_b[l], batch, seq)

        h = out_proj(out_a, out_b, ssq_a, ssq_b, w_o, l, h)

        n = rmsnorm(h, ffn2_norm[l], BF16)
        hid, w_down = ffn_up(n, ffn2_w_gate, ffn2_w_up, ffn2_w_down, l)
        h = matmul_ksplit_res(hid, w_down, h, 0.5)

    return rmsnorm(h, final_norm, F32).reshape(batch, seq, d)
```

```python
import functools
import math

import numpy as np
import jax
import jax.numpy as jnp
from jax import lax
from jax.experimental import pallas as pl
from jax.experimental.pallas import tpu as pltpu

F32 = jnp.float32
BF16 = jnp.bfloat16

HEAD_DIM = 128
N_HEADS = 16
WIDTH = N_HEADS * HEAD_DIM
DIL_CONFIGS = ((128, 1), (512, 4), (2048, 16))
ROPE_DIM = 64
Q_LORA = 1024
KV_LORA = 512
ROPE_THETA = 10000.0
EPS = 1e-6
MASK = -1e30
LOG2E = math.log2(math.e)

VMEM_LIMIT = 56 * 1024 * 1024
ATT_TQ = 512
ATT_TK = 512
MLA_KCHUNK = 8 * ATT_TK
DIL_MAX_DIST = max(w for w, _ in DIL_CONFIGS)
DIL_NOFF = DIL_MAX_DIST // ATT_TK + 1
DIL_KCHUNK = DIL_NOFF * ATT_TK


def _params(n_axes):
    return pltpu.CompilerParams(
        dimension_semantics=("arbitrary",) * n_axes, vmem_limit_bytes=VMEM_LIMIT)


def _rmsnorm_kernel(x_ref, g_ref, o_ref):
    x = x_ref[...]
    ms = jnp.mean(x * x, axis=-1, keepdims=True)
    o_ref[...] = ((x * lax.rsqrt(ms + EPS)) * g_ref[...]).astype(o_ref.dtype)


def rmsnorm(x, g, out_dtype, tm=512):
    m, d = x.shape
    return pl.pallas_call(
        _rmsnorm_kernel,
        grid=(m // tm,),
        in_specs=[pl.BlockSpec((tm, d), lambda i: (i, 0)),
                  pl.BlockSpec((1, d), lambda i: (0, 0))],
        out_specs=pl.BlockSpec((tm, d), lambda i: (i, 0)),
        out_shape=jax.ShapeDtypeStruct((m, d), out_dtype),
        compiler_params=_params(1),
        name="rmsnorm",
    )(x, g.reshape(1, d))


def _rstd(ssq_ref, dim):
    return lax.rsqrt(ssq_ref[...] * (1.0 / dim) + EPS)


def _dot(a, w):
    return jnp.dot(a, w.astype(BF16), preferred_element_type=F32)


def _dot_t(a, w_t):
    return lax.dot_general(a, w_t.astype(BF16), (((1,), (1,)), ((), ())), preferred_element_type=F32)


def _w_spec(w, layer, block, index_map):
    if w.ndim == 3:
        return pl.BlockSpec((None,) + block, lambda *g: (layer,) + index_map(*g))
    return pl.BlockSpec(block, index_map)


def _resident_spec(tm, kdim):
    return pl.BlockSpec((tm, kdim), lambda i, j: (i, 0), pipeline_mode=pl.Buffered(1))


def _mm_fullk_kernel(a_ref, w_ref, o_ref, *, scale, n_scaled):
    s = jnp.where(pl.program_id(1) < n_scaled, jnp.float32(scale), jnp.float32(1.0))
    o_ref[...] = (_dot(a_ref[...], w_ref[...]) * s).astype(o_ref.dtype)


def _out_proj_kernel(a_ref, b_ref, ssq_a_ref, ssq_b_ref, w_ref, r_ref, o_ref):
    ka = a_ref.shape[1]
    ya = _dot(a_ref[...], w_ref[:ka, :]) * _rstd(ssq_a_ref, ka)
    yb = _dot(b_ref[...], w_ref[ka:, :]) * _rstd(ssq_b_ref, b_ref.shape[1])
    o_ref[...] = r_ref[...] + ya + yb


def out_proj(a, b, ssq_a, ssq_b, w, layer, residual, tm=1024, tn=512):
    m, ka = a.shape
    kb = b.shape[1]
    n = w.shape[-1]
    tn = min(tn, n)
    tm = min(tm, m)
    rows = lambda width: pl.BlockSpec((tm, width), lambda i, j: (i, 0))
    o_spec = pl.BlockSpec((tm, tn), lambda i, j: (i, j))
    return pl.pallas_call(
        _out_proj_kernel,
        grid=(m // tm, n // tn),
        in_specs=[rows(ka), rows(kb), rows(1), rows(1),
                  _w_spec(w, layer, (ka + kb, tn), lambda i, j: (0, j)), o_spec],
        out_specs=o_spec,
        out_shape=jax.ShapeDtypeStruct((m, n), F32),
        compiler_params=_params(2),
        name="out_proj",
    )(a, b, ssq_a, ssq_b, w, residual)


def _mm_fullk_t_kernel(a_ref, ssq_ref, w_ref, o_ref, *, scale, n_scaled):
    s = jnp.where(pl.program_id(1) < n_scaled, jnp.float32(scale), jnp.float32(1.0))
    r = _rstd(ssq_ref, a_ref.shape[1]) * s
    o_ref[...] = (_dot_t(a_ref[...], w_ref[...]) * r).astype(o_ref.dtype)


def matmul_fullk_t(a, ssq, w_t, layer, *, out_dtype, n_cols, scale=1.0, n_scaled=0, tm=2048, tn=512):
    m, kdim = a.shape
    tm = min(tm, m)
    return pl.pallas_call(
        functools.partial(_mm_fullk_t_kernel, scale=scale, n_scaled=n_scaled),
        grid=(m // tm, n_cols // tn),
        in_specs=[_resident_spec(tm, kdim),
                  pl.BlockSpec((tm, 1), lambda i, j: (i, 0)),
                  pl.BlockSpec((None, tn, kdim), lambda i, j: (layer, j, 0))],
        out_specs=pl.BlockSpec((tm, tn), lambda i, j: (i, j)),
        out_shape=jax.ShapeDtypeStruct((m, n_cols), out_dtype),
        compiler_params=_params(2),
        name="matmul_fullk_t",
    )(a, ssq, w_t)


def matmul_fullk(a, w, layer, *, out_dtype, tm=2048, tn=256):
    m, kdim = a.shape
    n = w.shape[-1]
    tn = min(tn, n)
    tm = min(tm, m)
    return pl.pallas_call(
        functools.partial(_mm_fullk_kernel, scale=1.0, n_scaled=0),
        grid=(m // tm, n // tn),
        in_specs=[_resident_spec(tm, kdim), _w_spec(w, layer, (kdim, tn), lambda i, j: (0, j))],
        out_specs=pl.BlockSpec((tm, tn), lambda i, j: (i, j)),
        out_shape=jax.ShapeDtypeStruct((m, n), out_dtype),
        compiler_params=_params(2),
        name="matmul_fullk",
    )(a, w)


def _ffn_up_kernel(a_ref, wg_ref, wu_ref, wd_ref, o_ref, wd_bf16_ref, *, n_chunks):
    step = pl.program_id(0) * pl.num_programs(1) + pl.program_id(1)

    @pl.when(step < n_chunks)
    def _():
        wd_bf16_ref[...] = wd_ref[...].astype(BF16)

    a = a_ref[...]
    g = _dot(a, wg_ref[...])
    u = _dot(a, wu_ref[...])
    o_ref[...] = (g * jax.nn.sigmoid(g) * u).astype(o_ref.dtype)


def ffn_up(a, w_gate, w_up, w_down, layer, tm=2048, tn=256):
    m, kdim = a.shape
    n = w_gate.shape[-1]
    tm = min(tm, m)
    n_j = n // tn
    n_steps = (m // tm) * n_j
    chunk = next(c for c in range(128, n + 1, 128) if n % c == 0 and n // c <= n_steps)
    n_chunks = n // chunk
    d_out = w_down.shape[-1]
    chunk_of = lambda i, j: jnp.minimum(i * n_j + j, n_chunks - 1)
    w_spec = _w_spec(w_gate, layer, (kdim, tn), lambda i, j: (0, j))
    return pl.pallas_call(
        functools.partial(_ffn_up_kernel, n_chunks=n_chunks),
        grid=(m // tm, n_j),
        in_specs=[_resident_spec(tm, kdim), w_spec, w_spec,
                  pl.BlockSpec((None, chunk, d_out), lambda i, j: (layer, chunk_of(i, j), 0))],
        out_specs=[pl.BlockSpec((tm, tn), lambda i, j: (i, j)),
                   pl.BlockSpec((chunk, d_out), lambda i, j: (chunk_of(i, j), 0))],
        out_shape=[jax.ShapeDtypeStruct((m, n), BF16),
                   jax.ShapeDtypeStruct((n, d_out), BF16)],
        compiler_params=_params(2),
        name="ffn_up",
    )(a, w_gate, w_up, w_down)


def _mm_ksplit_res_kernel(a_ref, w_ref, r_ref, *rest, nk, k_rem, scale, emit):
    if emit:
        g_ref, o_ref, og_ref, ssq_ref, acc_ref = rest
    else:
        o_ref, acc_ref = rest
    k = pl.program_id(2)

    @pl.when(k == 0)
    def _():
        acc_ref[...] = _dot(a_ref[...], w_ref[...])

    @pl.when(jnp.logical_and(k > 0, k < nk - 1))
    def _():
        acc_ref[...] += _dot(a_ref[...], w_ref[...])

    @pl.when(k == nk - 1)
    def _():
        tail = _dot(a_ref[:, :k_rem], w_ref[:k_rem, :])
        o = r_ref[...] + scale * (acc_ref[...] + tail)
        o_ref[...] = o
        if emit:
            og_ref[...] = (o * g_ref[...]).astype(og_ref.dtype)
            part = jnp.sum(o * o, axis=-1, keepdims=True)
            j = pl.program_id(1)

            @pl.when(j == 0)
            def _():
                ssq_ref[...] = part

            @pl.when(j > 0)
            def _():
                ssq_ref[...] += part


def matmul_ksplit_res(a, w, residual, scale, g_next=None, tm=1024, tn=1024, tk=2816):
    m, kdim = a.shape
    n = w.shape[-1]
    tn = min(tn, n)
    tk = min(tk, kdim)
    nk = pl.cdiv(kdim, tk)
    assert nk >= 2
    k_rem = kdim - (nk - 1) * tk
    emit = g_next is not None
    o_spec = pl.BlockSpec((tm, tn), lambda i, j, k: (i, j))
    in_specs = [pl.BlockSpec((tm, tk), lambda i, j, k: (i, k)),
                pl.BlockSpec((tk, tn), lambda i, j, k: (k, j)),
                o_spec]
    args = (a, w, residual)
    out_specs, out_shape = o_spec, jax.ShapeDtypeStruct((m, n), F32)
    if emit:
        in_specs.append(pl.BlockSpec((1, tn), lambda i, j, k: (0, j)))
        args += (g_next.reshape(1, n),)
        out_specs = [o_spec, o_spec, pl.BlockSpec((tm, 1), lambda i, j, k: (i, 0))]
        out_shape = [out_shape, jax.ShapeDtypeStruct((m, n), BF16), jax.ShapeDtypeStruct((m, 1), F32)]
    return pl.pallas_call(
        functools.partial(_mm_ksplit_res_kernel, nk=nk, k_rem=k_rem, scale=scale, emit=emit),
        grid=(m // tm, n // tn, nk),
        in_specs=in_specs,
        out_specs=out_specs,
        out_shape=out_shape,
        scratch_shapes=[pltpu.VMEM((tm, tn), F32)],
        compiler_params=_params(3),
        name="matmul_ksplit_res",
    )(*args)


def _rope_slab(x, cos_p, sin_p):
    lane = lax.broadcasted_iota(jnp.int32, x.shape, 1)
    half = ROPE_DIM // 2
    swapped = jnp.where(lane < half, pltpu.roll(x, 128 - half, 1), pltpu.roll(x, half, 1))
    return x * cos_p + swapped * sin_p


def _proj_c_kernel(a_ref, ssq_ref, w_ref, gq_ref, gkv_ref, cos_ref, sin_ref,
                   cq_ref, ckv_ref, kr_ref, acc_ref, *, nk, dim):
    k = pl.program_id(1)

    @pl.when(k == 0)
    def _():
        acc_ref[...] = _dot_t(a_ref[...], w_ref[...])

    @pl.when(k > 0)
    def _():
        acc_ref[...] += _dot_t(a_ref[...], w_ref[...])

    @pl.when(k == nk - 1)
    def _():
        def norm(x, g):
            ms = jnp.mean(x * x, axis=-1, keepdims=True)
            return (x * lax.rsqrt(ms + EPS)) * g
        r = _rstd(ssq_ref, dim)
        cq_ref[...] = norm(acc_ref[:, :Q_LORA] * r, gq_ref[...]).astype(cq_ref.dtype)
        ckv_ref[...] = norm(acc_ref[:, Q_LORA:Q_LORA + KV_LORA] * r, gkv_ref[...]).astype(ckv_ref.dtype)
        kr = acc_ref[:, Q_LORA + KV_LORA:] * r
        kr_ref[...] = _rope_slab(kr, cos_ref[...], sin_ref[...]).astype(kr_ref.dtype)


def proj_c(a, ssq, w_c_t, g_cq, g_ckv, cos_p, sin_p, tm=1024, tk=1024):
    m, kdim = a.shape
    n = w_c_t.shape[0]
    tm = min(tm, m)
    tk = min(tk, kdim)
    nk = kdim // tk
    row = lambda width: pl.BlockSpec((tm, width), lambda i, k: (i, 0))
    vec = lambda width: pl.BlockSpec((1, width), lambda i, k: (0, 0))
    return pl.pallas_call(
        functools.partial(_proj_c_kernel, nk=nk, dim=kdim),
        grid=(m // tm, nk),
        in_specs=[pl.BlockSpec((tm, tk), lambda i, k: (i, k)),
                  row(1),
                  pl.BlockSpec((n, tk), lambda i, k: (0, k)),
                  vec(Q_LORA), vec(KV_LORA), row(128), row(128)],
        out_specs=[row(Q_LORA), row(KV_LORA), row(128)],
        out_shape=[jax.ShapeDtypeStruct((m, Q_LORA), BF16),
                   jax.ShapeDtypeStruct((m, KV_LORA), BF16),
                   jax.ShapeDtypeStruct((m, 128), BF16)],
        scratch_shapes=[pltpu.VMEM((tm, n), F32)],
        compiler_params=_params(2),
        name="proj_c",
    )(a, ssq, w_c_t, g_cq.reshape(1, Q_LORA), g_ckv.reshape(1, KV_LORA), cos_p, sin_p)


def _q_up_kernel(a_ref, w_ref, cos_ref, sin_ref, qn_ref, qr_ref, *, scale):
    x = _dot(a_ref[...], w_ref[...])
    cos_p = cos_ref[...]
    sin_p = sin_ref[...]
    low = lax.broadcasted_iota(jnp.int32, (x.shape[0], 128), 1) < ROPE_DIM
    for pair in range(x.shape[1] // 384):
        v0, v1, v2 = (x[:, (3 * pair + t) * 128:(3 * pair + t + 1) * 128] for t in range(3))
        v1r = pltpu.roll(v1, 64, 1)
        v2r = pltpu.roll(v2, 64, 1)
        heads = ((v0, jnp.where(low, v1, 0.0)),
                 (jnp.where(low, v1r, v2r), jnp.where(low, v2r, 0.0)))
        for t, (nope, rope) in enumerate(heads):
            cols = pl.ds((2 * pair + t) * 128, 128)
            qn_ref[:, cols] = (nope * scale).astype(qn_ref.dtype)
            qr_ref[:, cols] = (_rope_slab(rope, cos_p, sin_p) * scale).astype(qr_ref.dtype)


def q_up(a, w_uq, layer, cos_p, sin_p, scale, tm=2048, heads_per_tile=4):
    m, kdim = a.shape
    tm = min(tm, m)
    tn_in = heads_per_tile * (HEAD_DIM + ROPE_DIM)
    tn_out = heads_per_tile * 128
    o_spec = pl.BlockSpec((tm, tn_out), lambda i, j: (i, j))
    return pl.pallas_call(
        functools.partial(_q_up_kernel, scale=scale),
        grid=(m // tm, N_HEADS // heads_per_tile),
        in_specs=[pl.BlockSpec((tm, kdim), lambda i, j: (i, 0)),
                  pl.BlockSpec((None, kdim, tn_in), lambda i, j: (layer, 0, j)),
                  pl.BlockSpec((tm, 128), lambda i, j: (i, 0)),
                  pl.BlockSpec((tm, 128), lambda i, j: (i, 0))],
        out_specs=[o_spec, o_spec],
        out_shape=[jax.ShapeDtypeStruct((m, WIDTH), BF16), jax.ShapeDtypeStruct((m, WIDTH), BF16)],
        compiler_params=_params(2),
        name="q_up",
    )(a, w_uq, cos_p, sin_p)


def _qk(q, k):
    return lax.dot_general(q, k, (((1,), (1,)), ((), ())), preferred_element_type=F32)


def _softmax_update(s, v, m_ref, l_ref, acc_ref):
    m_prev = m_ref[...]
    m_new = jnp.maximum(m_prev, jnp.max(s, axis=-1, keepdims=True))
    alpha = jnp.exp2(m_prev - m_new)
    p = jnp.exp2(s - jnp.concatenate([m_new] * (s.shape[1] // 128), axis=1))
    l_ref[...] = alpha * l_ref[...] + jnp.sum(p, axis=-1, keepdims=True)
    acc_ref[...] = alpha * acc_ref[...] + jnp.dot(p.astype(BF16), v, preferred_element_type=F32)
    m_ref[...] = m_new


def _softmax_reset(m_ref, l_ref, acc_ref):
    m_ref[...] = jnp.full(m_ref.shape, MASK, F32)
    l_ref[...] = jnp.zeros(l_ref.shape, F32)
    acc_ref[...] = jnp.zeros(acc_ref.shape, F32)


def _softmax_scratch(nq):
    return [pltpu.VMEM((nq, ATT_TQ, 128), F32), pltpu.VMEM((nq, ATT_TQ, 128), F32),
            pltpu.VMEM((nq, ATT_TQ, HEAD_DIM), F32)]


def _head_block(seq, col):
    return pl.BlockSpec((seq, HEAD_DIM), col)


def _emit_head(rows, g_ref, og_ref, ssq_ref, l_ref, acc_ref):
    o = acc_ref[...] / l_ref[...]
    og_ref[rows, :] = (o * g_ref[...]).astype(og_ref.dtype)
    ssq_ref[rows, :] += jnp.sum(o * o, axis=-1, keepdims=True)


def _mla_kernel(qn_ref, qr_ref, kn_ref, kr_ref, v_ref, g_ref, og_ref, ssq_ref,
                kc_ref, m_ref, l_ref, acc_ref):
    @pl.when(pl.program_id(1) == 0)
    def _():
        ssq_ref[...] = jnp.zeros_like(ssq_ref)

    kc_ref[:, :HEAD_DIM] = kn_ref[...]
    kc_ref[:, HEAD_DIM:] = kr_ref[...]
    seq = og_ref.shape[0]
    for qi in range(seq // ATT_TQ):
        rows = pl.ds(qi * ATT_TQ, ATT_TQ)
        q = jnp.concatenate([qn_ref[rows, :], qr_ref[rows, :]], axis=-1)
        state = (m_ref.at[qi], l_ref.at[qi], acc_ref.at[qi])
        _softmax_reset(*state)
        n_keys = (qi + 1) * ATT_TK
        for k0 in range(0, n_keys, MLA_KCHUNK):
            width = min(MLA_KCHUNK, n_keys - k0)
            keys = pl.ds(k0, width)
            s = _qk(q, kc_ref[keys, :])
            if k0 + width == n_keys:
                row = lax.broadcasted_iota(jnp.int32, (ATT_TQ, width), 0) + qi * ATT_TQ
                col = lax.broadcasted_iota(jnp.int32, (ATT_TQ, width), 1) + k0
                s = jnp.where(row >= col, s, MASK)
            _softmax_update(s, v_ref[keys, :], *state)
        _emit_head(rows, g_ref, og_ref, ssq_ref, l_ref.at[qi], acc_ref.at[qi])


def _group_out(batch, seq):
    specs = [_head_block(seq, lambda b, h: (b, h)), pl.BlockSpec((seq, 1), lambda b, h: (b, 0))]
    shapes = [jax.ShapeDtypeStruct((batch * seq, WIDTH), BF16),
              jax.ShapeDtypeStruct((batch * seq, 1), F32)]
    return specs, shapes


def mla_attention(q_nope, q_rope, kv, k_r, g_out, batch, seq):
    out_specs, out_shape = _group_out(batch, seq)
    return pl.pallas_call(
        _mla_kernel,
        grid=(batch, N_HEADS),
        in_specs=[_head_block(seq, lambda b, h: (b, h)),
                  _head_block(seq, lambda b, h: (b, h)),
                  _head_block(seq, lambda b, h: (b, 2 * h)),
                  _head_block(seq, lambda b, h: (b, 0)),
                  _head_block(seq, lambda b, h: (b, 2 * h + 1)),
                  pl.BlockSpec((1, HEAD_DIM), lambda b, h: (0, h))],
        out_specs=out_specs,
        out_shape=out_shape,
        scratch_shapes=[pltpu.VMEM((seq, 2 * HEAD_DIM), BF16)] + _softmax_scratch(seq // ATT_TQ),
        compiler_params=_params(2),
        name="mla_attention",
    )(q_nope, q_rope, kv, k_r, kv, g_out.reshape(1, WIDTH))


def _dilated_distance(off):
    row = lax.broadcasted_iota(jnp.int32, (ATT_TQ, ATT_TK), 0)
    col = lax.broadcasted_iota(jnp.int32, (ATT_TQ, ATT_TK), 1)
    return row - col + off * ATT_TK


def _dilated_log2_mult(off):
    dist = _dilated_distance(off)
    mult = jnp.zeros((ATT_TQ, ATT_TK), jnp.int32)
    for window, dilation in DIL_CONFIGS:
        assert dilation & (dilation - 1) == 0
        held = (dist >= 0) & (dist <= window) & ((dist & (dilation - 1)) == 0)
        mult = mult + held.astype(jnp.int32)
    log2_mult = jnp.full((ATT_TQ, ATT_TK), MASK, F32)
    for n in range(1, len(DIL_CONFIGS) + 1):
        log2_mult = jnp.where(mult == n, jnp.float32(math.log2(n)), log2_mult)
    return log2_mult


def _dilated_kernel(slope_ref, q_ref, k_ref, v_ref, g_ref, og_ref, ssq_ref, lmult_ref, bias_ref,
                    m_ref, l_ref, acc_ref):
    @pl.when(jnp.logical_and(pl.program_id(0) == 0, pl.program_id(1) == 0))
    def _():
        for c in range(DIL_NOFF):
            lmult_ref[:, pl.ds(c * ATT_TK, ATT_TK)] = _dilated_log2_mult(DIL_NOFF - 1 - c)

    @pl.when(pl.program_id(1) == 0)
    def _():
        ssq_ref[...] = jnp.zeros_like(ssq_ref)

    neg_slope = -slope_ref[pl.program_id(1)] * LOG2E
    for c in range(DIL_NOFF):
        cols = pl.ds(c * ATT_TK, ATT_TK)
        bias_ref[:, cols] = neg_slope * _dilated_distance(DIL_NOFF - 1 - c).astype(F32) + lmult_ref[:, cols]
    seq = og_ref.shape[0]
    for qi in range(seq // ATT_TQ):
        rows = pl.ds(qi * ATT_TQ, ATT_TQ)
        q = q_ref[rows, :]
        state = (m_ref.at[qi], l_ref.at[qi], acc_ref.at[qi])
        _softmax_reset(*state)
        n_far = min(qi, DIL_NOFF - 1)
        n_keys = (n_far + 1) * ATT_TK
        for c0 in range(0, n_keys, DIL_KCHUNK):
            width = min(DIL_KCHUNK, n_keys - c0)
            keys = pl.ds((qi - n_far) * ATT_TK + c0, width)
            bias = bias_ref[:, pl.ds((DIL_NOFF - 1 - n_far) * ATT_TK + c0, width)]
            _softmax_update(_qk(q, k_ref[keys, :]) + bias, v_ref[keys, :], *state)
        _emit_head(rows, g_ref, og_ref, ssq_ref, l_ref.at[qi], acc_ref.at[qi])


def dilated_attention(qkv, slopes, g_out, batch, seq):
    out_specs, out_shape = _group_out(batch, seq)
    return pl.pallas_call(
        _dilated_kernel,
        grid=(batch, N_HEADS),
        in_specs=[pl.BlockSpec(memory_space=pltpu.SMEM),
                  _head_block(seq, lambda b, h: (b, h)),
                  _head_block(seq, lambda b, h: (b, N_HEADS + h)),
                  _head_block(seq, lambda b, h: (b, 2 * N_HEADS + h)),
                  pl.BlockSpec((1, HEAD_DIM), lambda b, h: (0, h))],
        out_specs=out_specs,
        out_shape=out_shape,
        scratch_shapes=[pltpu.VMEM((ATT_TQ, DIL_NOFF * ATT_TK), F32)] * 2 + _softmax_scratch(seq // ATT_TQ),
        compiler_params=_params(2),
        name="dilated_attention",
    )(slopes, qkv, qkv, qkv, g_out.reshape(1, WIDTH))


def kernel(x, positions, ffn1_norm, ffn1_w_gate, ffn1_w_up, ffn1_w_down, mix_norm, w_in, g_cq, w_uq,
           g_ckv, w_ukv, g_out_a, g_out_b, w_o, ffn2_norm, ffn2_w_gate, ffn2_w_up, ffn2_w_down,
           final_norm):
    batch, seq, d = x.shape
    m = batch * seq
    depth = w_in.shape[0]

    inv_freq = 1.0 / (ROPE_THETA ** (jnp.arange(0, ROPE_DIM, 2, dtype=F32) / ROPE_DIM))
    ang = positions.astype(F32)[..., None] * inv_freq
    cos = jnp.cos(ang).reshape(m, ROPE_DIM // 2)
    sin = jnp.sin(ang).reshape(m, ROPE_DIM // 2)
    zeros = jnp.zeros((m, 128 - ROPE_DIM), F32)
    cos_p = jnp.concatenate([cos, cos, zeros], axis=-1)
    sin_p = jnp.concatenate([-sin, sin, zeros], axis=-1)

    slopes = jnp.asarray(
        np.array([2.0 ** (-8.0 * (i + 1) / N_HEADS) for i in range(N_HEADS)], dtype=np.float32))
    scale_a = HEAD_DIM ** -0.5 * LOG2E
    scale_b = (HEAD_DIM + ROPE_DIM) ** -0.5 * LOG2E

    w_in_t = jnp.swapaxes(w_in, 1, 2)

    h = x.reshape(m, d)
    for l in range(depth):
        n = rmsnorm(h, ffn1_norm[l], BF16)
        hid, w_down = ffn_up(n, ffn1_w_gate, ffn1_w_up, ffn1_w_down, l)
        h, hg, ssq = matmul_ksplit_res(hid, w_down, h, 0.5, g_next=mix_norm[l])

        qkv_a = matmul_fullk_t(hg, ssq, w_in_t, l, out_dtype=BF16, n_cols=3 * WIDTH, scale=scale_a,
                               n_scaled=WIDTH // 512)
        w_c_t = jnp.concatenate(
            [w_in_t[l, 3 * WIDTH:, :], jnp.zeros((128 - ROPE_DIM, d), F32)], axis=0)
        cq_n, ckv_n, k_r = proj_c(hg, ssq, w_c_t, g_cq[l], g_ckv[l], cos_p, sin_p)

        out_a, ssq_a = dilated_attention(qkv_a, slopes, g_out_a[l], batch, seq)

        q_nope, q_rope = q_up(cq_n, w_uq, l, cos_p, sin_p, scale_b)
        kv = matmul_fullk(ckv_n, w_ukv, l, out_dtype=BF16, tn=1024)
        out_b, ssq_b = mla_attention(q_nope, q_rope, kv, k_r, g_out_b[l], batch, seq)

        h = out_proj(out_a, out_b, ssq_a, ssq_b, w_o, l, h)

        n = rmsnorm(h, ffn2_norm[l], BF16)
        hid, w_down = ffn_up(n, ffn2_w_gate, ffn2_w_up, ffn2_w_down, l)
        h = matmul_ksplit_res(hid, w_down, h, 0.5)

    return rmsnorm(h, final_norm, F32).reshape(batch, seq, d)
```

```python
import functools
import math

import numpy as np
import jax
import jax.numpy as jnp
from jax import lax
from jax.experimental import pallas as pl
from jax.experimental.pallas import tpu as pltpu

F32 = jnp.float32
BF16 = jnp.bfloat16

HEAD_DIM = 128
N_HEADS = 16
WIDTH = N_HEADS * HEAD_DIM
DIL_CONFIGS = ((128, 1), (512, 4), (2048, 16))
ROPE_DIM = 64
Q_LORA = 1024
KV_LORA = 512
ROPE_THETA = 10000.0
EPS = 1e-6
MASK = -1e30
LOG2E = math.log2(math.e)

VMEM_LIMIT = 56 * 1024 * 1024
ATT_TQ = 512
ATT_TK = 512
MLA_KCHUNK = 8 * ATT_TK
DIL_MAX_DIST = max(w for w, _ in DIL_CONFIGS)
DIL_NOFF = DIL_MAX_DIST // ATT_TK + 1
DIL_KCHUNK = DIL_NOFF * ATT_TK


def _params(n_axes):
    return pltpu.CompilerParams(
        dimension_semantics=("arbitrary",) * n_axes, vmem_limit_bytes=VMEM_LIMIT)


def _rmsnorm_kernel(x_ref, g_ref, o_ref):
    x = x_ref[...]
    ms = jnp.mean(x * x, axis=-1, keepdims=True)
    o_ref[...] = ((x * lax.rsqrt(ms + EPS)) * g_ref[...]).astype(o_ref.dtype)


def rmsnorm(x, g, out_dtype, tm=512):
    m, d = x.shape
    return pl.pallas_call(
        _rmsnorm_kernel,
        grid=(m // tm,),
        in_specs=[pl.BlockSpec((tm, d), lambda i: (i, 0)),
                  pl.BlockSpec((1, d), lambda i: (0, 0))],
        out_specs=pl.BlockSpec((tm, d), lambda i: (i, 0)),
        out_shape=jax.ShapeDtypeStruct((m, d), out_dtype),
        compiler_params=_params(1),
        name="rmsnorm",
    )(x, g.reshape(1, d))


def _rstd(ssq_ref, dim):
    return lax.rsqrt(ssq_ref[...] * (1.0 / dim) + EPS)


def _dot(a, w):
    return jnp.dot(a, w.astype(BF16), preferred_element_type=F32)


def _dot_t(a, w_t):
    return lax.dot_general(a, w_t.astype(BF16), (((1,), (1,)), ((), ())), preferred_element_type=F32)


def _w_spec(w, layer, block, index_map):
    if w.ndim == 3:
        return pl.BlockSpec((None,) + block, lambda *g: (layer,) + index_map(*g))
    return pl.BlockSpec(block, index_map)


def _resident_spec(tm, kdim):
    return pl.BlockSpec((tm, kdim), lambda i, j: (i, 0), pipeline_mode=pl.Buffered(1))


def _mm_fullk_kernel(a_ref, w_ref, o_ref, *, scale, n_scaled):
    s = jnp.where(pl.program_id(1) < n_scaled, jnp.float32(scale), jnp.float32(1.0))
    o_ref[...] = (_dot(a_ref[...], w_ref[...]) * s).astype(o_ref.dtype)


def _out_proj_kernel(a_ref, b_ref, ssq_a_ref, ssq_b_ref, w_ref, r_ref, o_ref):
    ka = a_ref.shape[1]
    ya = _dot(a_ref[...], w_ref[:ka, :]) * _rstd(ssq_a_ref, ka)
    yb = _dot(b_ref[...], w_ref[ka:, :]) * _rstd(ssq_b_ref, b_ref.shape[1])
    o_ref[...] = r_ref[...] + ya + yb


def out_proj(a, b, ssq_a, ssq_b, w, layer, residual, tm=1024, tn=512):
    m, ka = a.shape
    kb = b.shape[1]
    n = w.shape[-1]
    tn = min(tn, n)
    tm = min(tm, m)
    rows = lambda width: pl.BlockSpec((tm, width), lambda i, j: (i, 0))
    o_spec = pl.BlockSpec((tm, tn), lambda i, j: (i, j))
    return pl.pallas_call(
        _out_proj_kernel,
        grid=(m // tm, n // tn),
        in_specs=[rows(ka), rows(kb), rows(1), rows(1),
                  _w_spec(w, layer, (ka + kb, tn), lambda i, j: (0, j)), o_spec],
        out_specs=o_spec,
        out_shape=jax.ShapeDtypeStruct((m, n), F32),
        compiler_params=_params(2),
        name="out_proj",
    )(a, b, ssq_a, ssq_b, w, residual)


def _mm_fullk_t_kernel(a_ref, ssq_ref, w_ref, o_ref, *, scale, n_scaled):
    s = jnp.where(pl.program_id(1) < n_scaled, jnp.float32(scale), jnp.float32(1.0))
    r = _rstd(ssq_ref, a_ref.shape[1]) * s
    o_ref[...] = (_dot_t(a_ref[...], w_ref[...]) * r).astype(o_ref.dtype)


def matmul_fullk_t(a, ssq, w_t, layer, *, out_dtype, n_cols, scale=1.0, n_scaled=0, tm=2048, tn=512):
    m, kdim = a.shape
    tm = min(tm, m)
    return pl.pallas_call(
        functools.partial(_mm_fullk_t_kernel, scale=scale, n_scaled=n_scaled),
        grid=(m // tm, n_cols // tn),
        in_specs=[_resident_spec(tm, kdim),
                  pl.BlockSpec((tm, 1), lambda i, j: (i, 0)),
                  pl.BlockSpec((None, tn, kdim), lambda i, j: (layer, j, 0))],
        out_specs=pl.BlockSpec((tm, tn), lambda i, j: (i, j)),
        out_shape=jax.ShapeDtypeStruct((m, n_cols), out_dtype),
        compiler_params=_params(2),
        name="matmul_fullk_t",
    )(a, ssq, w_t)


def matmul_fullk(a, w, layer, *, out_dtype, tm=2048, tn=256):
    m, kdim = a.shape
    n = w.shape[-1]
    tn = min(tn, n)
    tm = min(tm, m)
    return pl.pallas_call(
        functools.partial(_mm_fullk_kernel, scale=1.0, n_scaled=0),
        grid=(m // tm, n // tn),
        in_specs=[_resident_spec(tm, kdim), _w_spec(w, layer, (kdim, tn), lambda i, j: (0, j))],
        out_specs=pl.BlockSpec((tm, tn), lambda i, j: (i, j)),
        out_shape=jax.ShapeDtypeStruct((m, n), out_dtype),
        compiler_params=_params(2),
        name="matmul_fullk",
    )(a, w)


def _ffn_up_kernel(a_ref, wg_ref, wu_ref, wd_ref, o_ref, wd_bf16_ref, *, n_chunks):
    step = pl.program_id(0) * pl.num_programs(1) + pl.program_id(1)

    @pl.when(step < n_chunks)
    def _():
        wd_bf16_ref[...] = wd_ref[...].astype(BF16)

    a = a_ref[...]
    g = _dot(a, wg_ref[...])
    u = _dot(a, wu_ref[...])
    o_ref[...] = (g * jax.nn.sigmoid(g) * u).astype(o_ref.dtype)


def ffn_up(a, w_gate, w_up, w_down, layer, tm=2048, tn=256):
    m, kdim = a.shape
    n = w_gate.shape[-1]
    tm = min(tm, m)
    n_j = n // tn
    n_steps = (m // tm) * n_j
    chunk = next(c for c in range(128, n + 1, 128) if n % c == 0 and n // c <= n_steps)
    n_chunks = n // chunk
    d_out = w_down.shape[-1]
    chunk_of = lambda i, j: jnp.minimum(i * n_j + j, n_chunks - 1)
    w_spec = _w_spec(w_gate, layer, (kdim, tn), lambda i, j: (0, j))
    return pl.pallas_call(
        functools.partial(_ffn_up_kernel, n_chunks=n_chunks),
        grid=(m // tm, n_j),
        in_specs=[_resident_spec(tm, kdim), w_spec, w_spec,
                  pl.BlockSpec((None, chunk, d_out), lambda i, j: (layer, chunk_of(i, j), 0))],
        out_specs=[pl.BlockSpec((tm, tn), lambda i, j: (i, j)),
                   pl.BlockSpec((chunk, d_out), lambda i, j: (chunk_of(i, j), 0))],
        out_shape=[jax.ShapeDtypeStruct((m, n), BF16),
                   jax.ShapeDtypeStruct((n, d_out), BF16)],
        compiler_params=_params(2),
        name="ffn_up",
    )(a, w_gate, w_up, w_down)


def _mm_ksplit_res_kernel(a_ref, w_ref, r_ref, *rest, nk, k_rem, scale, emit):
    if emit:
        g_ref, o_ref, og_ref, ssq_ref, acc_ref = rest
    else:
        o_ref, acc_ref = rest
    k = pl.program_id(2)

    @pl.when(k == 0)
    def _():
        acc_ref[...] = _dot(a_ref[...], w_ref[...])

    @pl.when(jnp.logical_and(k > 0, k < nk - 1))
    def _():
        acc_ref[...] += _dot(a_ref[...], w_ref[...])

    @pl.when(k == nk - 1)
    def _():
        tail = _dot(a_ref[:, :k_rem], w_ref[:k_rem, :])
        o = r_ref[...] + scale * (acc_ref[...] + tail)
        o_ref[...] = o
        if emit:
            og_ref[...] = (o * g_ref[...]).astype(og_ref.dtype)
            part = jnp.sum(o * o, axis=-1, keepdims=True)
            j = pl.program_id(1)

            @pl.when(j == 0)
            def _():
                ssq_ref[...] = part

            @pl.when(j > 0)
            def _():
                ssq_ref[...] += part


def matmul_ksplit_res(a, w, residual, scale, g_next=None, tm=1024, tn=1024, tk=2816):
    m, kdim = a.shape
    n = w.shape[-1]
    tn = min(tn, n)
    tk = min(tk, kdim)
    nk = pl.cdiv(kdim, tk)
    assert nk >= 2
    k_rem = kdim - (nk - 1) * tk
    emit = g_next is not None
    o_spec = pl.BlockSpec((tm, tn), lambda i, j, k: (i, j))
    in_specs = [pl.BlockSpec((tm, tk), lambda i, j, k: (i, k)),
                pl.BlockSpec((tk, tn), lambda i, j, k: (k, j)),
                o_spec]
    args = (a, w, residual)
    out_specs, out_shape = o_spec, jax.ShapeDtypeStruct((m, n), F32)
    if emit:
        in_specs.append(pl.BlockSpec((1, tn), lambda i, j, k: (0, j)))
        args += (g_next.reshape(1, n),)
        out_specs = [o_spec, o_spec, pl.BlockSpec((tm, 1), lambda i, j, k: (i, 0))]
        out_shape = [out_shape, jax.ShapeDtypeStruct((m, n), BF16), jax.ShapeDtypeStruct((m, 1), F32)]
    return pl.pallas_call(
        functools.partial(_mm_ksplit_res_kernel, nk=nk, k_rem=k_rem, scale=scale, emit=emit),
        grid=(m // tm, n // tn, nk),
        in_specs=in_specs,
        out_specs=out_specs,
        out_shape=out_shape,
        scratch_shapes=[pltpu.VMEM((tm, tn), F32)],
        compiler_params=_params(3),
        name="matmul_ksplit_res",
    )(*args)


def _rope_slab(x, cos_p, sin_p):
    lane = lax.broadcasted_iota(jnp.int32, x.shape, 1)
    half = ROPE_DIM // 2
    swapped = jnp.where(lane < half, pltpu.roll(x, 128 - half, 1), pltpu.roll(x, half, 1))
    return x * cos_p + swapped * sin_p


def _proj_c_kernel(a_ref, ssq_ref, w_ref, gq_ref, gkv_ref, cos_ref, sin_ref,
                   cq_ref, ckv_ref, kr_ref, acc_ref, *, nk, dim):
    k = pl.program_id(1)

    @pl.when(k == 0)
    def _():
        acc_ref[...] = _dot_t(a_ref[...], w_ref[...])

    @pl.when(k > 0)
    def _():
        acc_ref[...] += _dot_t(a_ref[...], w_ref[...])

    @pl.when(k == nk - 1)
    def _():
        def norm(x, g):
            ms = jnp.mean(x * x, axis=-1, keepdims=True)
            return (x * lax.rsqrt(ms + EPS)) * g
        r = _rstd(ssq_ref, dim)
        cq_ref[...] = norm(acc_ref[:, :Q_LORA] * r, gq_ref[...]).astype(cq_ref.dtype)
        ckv_ref[...] = norm(acc_ref[:, Q_LORA:Q_LORA + KV_LORA] * r, gkv_ref[...]).astype(ckv_ref.dtype)
        kr = acc_ref[:, Q_LORA + KV_LORA:] * r
        kr_ref[...] = _rope_slab(kr, cos_ref[...], sin_ref[...]).astype(kr_ref.dtype)


def proj_c(a, ssq, w_c_t, g_cq, g_ckv, cos_p, sin_p, tm=1024, tk=1024):
    m, kdim = a.shape
    n = w_c_t.shape[0]
    tm = min(tm, m)
    tk = min(tk, kdim)
    nk = kdim // tk
    row = lambda width: pl.BlockSpec((tm, width), lambda i, k: (i, 0))
    vec = lambda width: pl.BlockSpec((1, width), lambda i, k: (0, 0))
    return pl.pallas_call(
        functools.partial(_proj_c_kernel, nk=nk, dim=kdim),
        grid=(m // tm, nk),
        in_specs=[pl.BlockSpec((tm, tk), lambda i, k: (i, k)),
                  row(1),
                  pl.BlockSpec((n, tk), lambda i, k: (0, k)),
                  vec(Q_LORA), vec(KV_LORA), row(128), row(128)],
        out_specs=[row(Q_LORA), row(KV_LORA), row(128)],
        out_shape=[jax.ShapeDtypeStruct((m, Q_LORA), BF16),
                   jax.ShapeDtypeStruct((m, KV_LORA), BF16),
                   jax.ShapeDtypeStruct((m, 128), BF16)],
        scratch_shapes=[pltpu.VMEM((tm, n), F32)],
        compiler_params=_params(2),
        name="proj_c",
    )(a, ssq, w_c_t, g_cq.reshape(1, Q_LORA), g_ckv.reshape(1, KV_LORA), cos_p, sin_p)


def _q_up_kernel(a_ref, w_ref, cos_ref, sin_ref, qn_ref, qr_ref, *, scale):
    x = _dot(a_ref[...], w_ref[...])
    cos_p = cos_ref[...]
    sin_p = sin_ref[...]
    low = lax.broadcasted_iota(jnp.int32, (x.shape[0], 128), 1) < ROPE_DIM
    for pair in range(x.shape[1] // 384):
        v0, v1, v2 = (x[:, (3 * pair + t) * 128:(3 * pair + t + 1) * 128] for t in range(3))
        v1r = pltpu.roll(v1, 64, 1)
        v2r = pltpu.roll(v2, 64, 1)
        heads = ((v0, jnp.where(low, v1, 0.0)),
                 (jnp.where(low, v1r, v2r), jnp.where(low, v2r, 0.0)))
        for t, (nope, rope) in enumerate(heads):
            cols = pl.ds((2 * pair + t) * 128, 128)
            qn_ref[:, cols] = (nope * scale).astype(qn_ref.dtype)
            qr_ref[:, cols] = (_rope_slab(rope, cos_p, sin_p) * scale).astype(qr_ref.dtype)


def q_up(a, w_uq, layer, cos_p, sin_p, scale, tm=2048, heads_per_tile=4):
    m, kdim = a.shape
    tm = min(tm, m)
    tn_in = heads_per_tile * (HEAD_DIM + ROPE_DIM)
    tn_out = heads_per_tile * 128
    o_spec = pl.BlockSpec((tm, tn_out), lambda i, j: (i, j))
    return pl.pallas_call(
        functools.partial(_q_up_kernel, scale=scale),
        grid=(m // tm, N_HEADS // heads_per_tile),
        in_specs=[pl.BlockSpec((tm, kdim), lambda i, j: (i, 0)),
                  pl.BlockSpec((None, kdim, tn_in), lambda i, j: (layer, 0, j)),
                  pl.BlockSpec((tm, 128), lambda i, j: (i, 0)),
                  pl.BlockSpec((tm, 128), lambda i, j: (i, 0))],
        out_specs=[o_spec, o_spec],
        out_shape=[jax.ShapeDtypeStruct((m, WIDTH), BF16), jax.ShapeDtypeStruct((m, WIDTH), BF16)],
        compiler_params=_params(2),
        name="q_up",
    )(a, w_uq, cos_p, sin_p)


def _qk(q, k):
    return lax.dot_general(q, k, (((1,), (1,)), ((), ())), preferred_element_type=F32)


def _softmax_update(s, v1, m_ref, acc_ref):
    m_prev = m_ref[...]
    m_new = jnp.maximum(m_prev, jnp.max(s, axis=-1, keepdims=True))
    alpha = jnp.exp2(m_prev - m_new)
    p = jnp.exp2(s - jnp.concatenate([m_new] * (s.shape[1] // 128), axis=1))
    pv = jnp.dot(p.astype(BF16), v1, preferred_element_type=F32)
    acc_ref[...] = jnp.concatenate([alpha, alpha], axis=1) * acc_ref[...] + pv
    m_ref[...] = m_new


def _softmax_reset(m_ref, acc_ref):
    m_ref[...] = jnp.full(m_ref.shape, MASK, F32)
    acc_ref[...] = jnp.zeros(acc_ref.shape, F32)


def _softmax_scratch(seq):
    nq = seq // ATT_TQ
    return [pltpu.VMEM((seq, 2 * HEAD_DIM), BF16), pltpu.VMEM((nq, ATT_TQ, 128), F32),
            pltpu.VMEM((nq, ATT_TQ, 2 * HEAD_DIM), F32)]


def _values_with_ones(v_ref, v1_ref):
    v1_ref[:, :HEAD_DIM] = v_ref[...]
    v1_ref[:, HEAD_DIM:] = jnp.ones(v_ref.shape, v1_ref.dtype)


def _head_block(seq, col):
    return pl.BlockSpec((seq, HEAD_DIM), col)


def _emit_head(rows, g_ref, og_ref, ssq_ref, acc_ref):
    acc = acc_ref[...]
    o = acc[:, :HEAD_DIM] / acc[:, HEAD_DIM:]
    og_ref[rows, :] = (o * g_ref[...]).astype(og_ref.dtype)
    ssq_ref[rows, :] += jnp.sum(o * o, axis=-1, keepdims=True)


def _mla_kernel(qn_ref, qr_ref, kn_ref, kr_ref, v_ref, g_ref, og_ref, ssq_ref,
                kc_ref, v1_ref, m_ref, acc_ref):
    @pl.when(pl.program_id(1) == 0)
    def _():
        ssq_ref[...] = jnp.zeros_like(ssq_ref)

    kc_ref[:, :HEAD_DIM] = kn_ref[...]
    kc_ref[:, HEAD_DIM:] = kr_ref[...]
    _values_with_ones(v_ref, v1_ref)
    seq = og_ref.shape[0]
    for qi in range(seq // ATT_TQ):
        rows = pl.ds(qi * ATT_TQ, ATT_TQ)
        q = jnp.concatenate([qn_ref[rows, :], qr_ref[rows, :]], axis=-1)
        state = (m_ref.at[qi], acc_ref.at[qi])
        _softmax_reset(*state)
        n_keys = (qi + 1) * ATT_TK
        for k0 in range(0, n_keys, MLA_KCHUNK):
            width = min(MLA_KCHUNK, n_keys - k0)
            keys = pl.ds(k0, width)
            s = _qk(q, kc_ref[keys, :])
            if k0 + width == n_keys:
                row = lax.broadcasted_iota(jnp.int32, (ATT_TQ, width), 0) + qi * ATT_TQ
                col = lax.broadcasted_iota(jnp.int32, (ATT_TQ, width), 1) + k0
                s = jnp.where(row >= col, s, MASK)
            _softmax_update(s, v1_ref[keys, :], *state)
        _emit_head(rows, g_ref, og_ref, ssq_ref, acc_ref.at[qi])


def _group_out(batch, seq):
    specs = [_head_block(seq, lambda b, h: (b, h)), pl.BlockSpec((seq, 1), lambda b, h: (b, 0))]
    shapes = [jax.ShapeDtypeStruct((batch * seq, WIDTH), BF16),
              jax.ShapeDtypeStruct((batch * seq, 1), F32)]
    return specs, shapes


def mla_attention(q_nope, q_rope, kv, k_r, g_out, batch, seq):
    out_specs, out_shape = _group_out(batch, seq)
    return pl.pallas_call(
        _mla_kernel,
        grid=(batch, N_HEADS),
        in_specs=[_head_block(seq, lambda b, h: (b, h)),
                  _head_block(seq, lambda b, h: (b, h)),
                  _head_block(seq, lambda b, h: (b, 2 * h)),
                  _head_block(seq, lambda b, h: (b, 0)),
                  _head_block(seq, lambda b, h: (b, 2 * h + 1)),
                  pl.BlockSpec((1, HEAD_DIM), lambda b, h: (0, h))],
        out_specs=out_specs,
        out_shape=out_shape,
        scratch_shapes=[pltpu.VMEM((seq, 2 * HEAD_DIM), BF16)] + _softmax_scratch(seq),
        compiler_params=_params(2),
        name="mla_attention",
    )(q_nope, q_rope, kv, k_r, kv, g_out.reshape(1, WIDTH))


def _dilated_distance(off):
    row = lax.broadcasted_iota(jnp.int32, (ATT_TQ, ATT_TK), 0)
    col = lax.broadcasted_iota(jnp.int32, (ATT_TQ, ATT_TK), 1)
    return row - col + off * ATT_TK


def _dilated_log2_mult(off):
    dist = _dilated_distance(off)
    mult = jnp.zeros((ATT_TQ, ATT_TK), jnp.int32)
    for window, dilation in DIL_CONFIGS:
        assert dilation & (dilation - 1) == 0
        held = (dist >= 0) & (dist <= window) & ((dist & (dilation - 1)) == 0)
        mult = mult + held.astype(jnp.int32)
    log2_mult = jnp.full((ATT_TQ, ATT_TK), MASK, F32)
    for n in range(1, len(DIL_CONFIGS) + 1):
        log2_mult = jnp.where(mult == n, jnp.float32(math.log2(n)), log2_mult)
    return log2_mult


def _dilated_kernel(slope_ref, q_ref, k_ref, v_ref, g_ref, og_ref, ssq_ref, lmult_ref, bias_ref,
                    v1_ref, m_ref, acc_ref):
    @pl.when(jnp.logical_and(pl.program_id(0) == 0, pl.program_id(1) == 0))
    def _():
        for c in range(DIL_NOFF):
            lmult_ref[:, pl.ds(c * ATT_TK, ATT_TK)] = _dilated_log2_mult(DIL_NOFF - 1 - c)

    @pl.when(pl.program_id(1) == 0)
    def _():
        ssq_ref[...] = jnp.zeros_like(ssq_ref)

    _values_with_ones(v_ref, v1_ref)
    neg_slope = -slope_ref[pl.program_id(1)] * LOG2E
    for c in range(DIL_NOFF):
        cols = pl.ds(c * ATT_TK, ATT_TK)
        bias_ref[:, cols] = neg_slope * _dilated_distance(DIL_NOFF - 1 - c).astype(F32) + lmult_ref[:, cols]
    seq = og_ref.shape[0]
    for qi in range(seq // ATT_TQ):
        rows = pl.ds(qi * ATT_TQ, ATT_TQ)
        q = q_ref[rows, :]
        state = (m_ref.at[qi], acc_ref.at[qi])
        _softmax_reset(*state)
        n_far = min(qi, DIL_NOFF - 1)
        n_keys = (n_far + 1) * ATT_TK
        for c0 in range(0, n_keys, DIL_KCHUNK):
            width = min(DIL_KCHUNK, n_keys - c0)
            keys = pl.ds((qi - n_far) * ATT_TK + c0, width)
            bias = bias_ref[:, pl.ds((DIL_NOFF - 1 - n_far) * ATT_TK + c0, width)]
            _softmax_update(_qk(q, k_ref[keys, :]) + bias, v1_ref[keys, :], *state)
        _emit_head(rows, g_ref, og_ref, ssq_ref, acc_ref.at[qi])


def dilated_attention(qkv, slopes, g_out, batch, seq):
    out_specs, out_shape = _group_out(batch, seq)
    return pl.pallas_call(
        _dilated_kernel,
        grid=(batch, N_HEADS),
        in_specs=[pl.BlockSpec(memory_space=pltpu.SMEM),
                  _head_block(seq, lambda b, h: (b, h)),
                  _head_block(seq, lambda b, h: (b, N_HEADS + h)),
                  _head_block(seq, lambda b, h: (b, 2 * N_HEADS + h)),
                  pl.BlockSpec((1, HEAD_DIM), lambda b, h: (0, h))],
        out_specs=out_specs,
        out_shape=out_shape,
        scratch_shapes=[pltpu.VMEM((ATT_TQ, DIL_NOFF * ATT_TK), F32)] * 2 + _softmax_scratch(seq),
        compiler_params=_params(2),
        name="dilated_attention",
    )(slopes, qkv, qkv, qkv, g_out.reshape(1, WIDTH))


def kernel(x, positions, ffn1_norm, ffn1_w_gate, ffn1_w_up, ffn1_w_down, mix_norm, w_in, g_cq, w_uq,
           g_ckv, w_ukv, g_out_a, g_out_b, w_o, ffn2_norm, ffn2_w_gate, ffn2_w_up, ffn2_w_down,
           final_norm):
    batch, seq, d = x.shape
    m = batch * seq
    depth = w_in.shape[0]

    inv_freq = 1.0 / (ROPE_THETA ** (jnp.arange(0, ROPE_DIM, 2, dtype=F32) / ROPE_DIM))
    ang = positions.astype(F32)[..., None] * inv_freq
    cos = jnp.cos(ang).reshape(m, ROPE_DIM // 2)
    sin = jnp.sin(ang).reshape(m, ROPE_DIM // 2)
    zeros = jnp.zeros((m, 128 - ROPE_DIM), F32)
    cos_p = jnp.concatenate([cos, cos, zeros], axis=-1)
    sin_p = jnp.concatenate([-sin, sin, zeros], axis=-1)

    slopes = jnp.asarray(
        np.array([2.0 ** (-8.0 * (i + 1) / N_HEADS) for i in range(N_HEADS)], dtype=np.float32))
    scale_a = HEAD_DIM ** -0.5 * LOG2E
    scale_b = (HEAD_DIM + ROPE_DIM) ** -0.5 * LOG2E

    w_in_t = jnp.swapaxes(w_in, 1, 2)

    h = x.reshape(m, d)
    for l in range(depth):
        n = rmsnorm(h, ffn1_norm[l], BF16)
        hid, w_down = ffn_up(n, ffn1_w_gate, ffn1_w_up, ffn1_w_down, l)
        h, hg, ssq = matmul_ksplit_res(hid, w_down, h, 0.5, g_next=mix_norm[l])

        qkv_a = matmul_fullk_t(hg, ssq, w_in_t, l, out_dtype=BF16, n_cols=3 * WIDTH, scale=scale_a,
                               n_scaled=WIDTH // 512)
        w_c_t = jnp.concatenate(
            [w_in_t[l, 3 * WIDTH:, :], jnp.zeros((128 - ROPE_DIM, d), F32)], axis=0)
        cq_n, ckv_n, k_r = proj_c(hg, ssq, w_c_t, g_cq[l], g_ckv[l], cos_p, sin_p)

        out_a, ssq_a = dilated_attention(qkv_a, slopes, g_out_a[l], batch, seq)

        q_nope, q_rope = q_up(cq_n, w_uq, l, cos_p, sin_p, scale_b)
        kv = matmul_fullk(ckv_n, w_ukv, l, out_dtype=BF16, tn=1024)
        out_b, ssq_b = mla_attention(q_nope, q_rope, kv, k_r, g_out_b[l], batch, seq)

        h = out_proj(out_a, out_b, ssq_a, ssq_b, w_o, l, h)

        n = rmsnorm(h, ffn2_norm[l], BF16)
        hid, w_down = ffn_up(n, ffn2_w_gate, ffn2_w_up, ffn2_w_down, l)
        h = matmul_ksplit_res(hid, w_down, h, 0.5)

    return rmsnorm(h, final_norm, F32).reshape(batch, seq, d)
```
